```python
import math
import jax, jax.numpy as jnp
from jax import lax
import numpy as np

D_MODEL = 1024
BATCH = 32
SEQ = 2048
DEPTH = 1

SSM_GROUP = 16
SSM_WIDTH = D_MODEL // 2
SSM_GROUPS = SSM_WIDTH // SSM_GROUP
SSM_STATE = 64
CONV_WIDTH = D_MODEL
CONV_K = 3
D_FF = 4 * D_MODEL
NORM_EPS = 1e-6
DT_MIN = 1e-3
DT_MAX = 1e-1
SPLIT_SIZES = (SSM_WIDTH, CONV_WIDTH, CONV_WIDTH, CONV_WIDTH, D_MODEL, D_MODEL)
IN_COLS = sum(SPLIT_SIZES)
SPLIT_POINTS = tuple(int(v) for v in np.cumsum(SPLIT_SIZES)[:-1])

kernel_name = "hybrid_s5_shortconv_gated_block"


def rmsnorm(x, g):
    xf = x.astype(jnp.float32)
    var = jnp.mean(xf * xf, axis=-1, keepdims=True)
    return (xf * lax.rsqrt(var + NORM_EPS) * g.astype(jnp.float32)).astype(x.dtype)


def _ssm_combine(e1, e2):
    a1r, a1i, b1r, b1i = e1
    a2r, a2i, b2r, b2i = e2
    ar = a1r * a2r - a1i * a2i
    ai = a1r * a2i + a1i * a2r
    br = a2r * b1r - a2i * b1i + b2r
    bi = a2r * b1i + a2i * b1r + b2i
    return (ar, ai, br, bi)


def s5_branch(u, lam_re, lam_im, log_dt, b_re, b_im, c_re, c_im, d_skip):
    f32 = jnp.float32
    bsz, seq, _ = u.shape
    uf = u.astype(f32).reshape(bsz, seq, SSM_GROUPS, SSM_GROUP)
    lr = lam_re.astype(f32)
    li = lam_im.astype(f32)
    dt = jnp.exp(log_dt.astype(f32))[:, None]
    mag = jnp.exp(lr * dt)
    ab_re = mag * jnp.cos(li * dt)
    ab_im = mag * jnp.sin(li * dt)
    er = ab_re - 1.0
    ei = ab_im
    den = lr * lr + li * li
    q_re = (er * lr + ei * li) / den
    q_im = (ei * lr - er * li) / den
    br = b_re.astype(f32)
    bi = b_im.astype(f32)
    bb_re = q_re[..., None] * br - q_im[..., None] * bi
    bb_im = q_re[..., None] * bi + q_im[..., None] * br
    bu_re = jnp.einsum('gnc,bsgc->bsgn', bb_re, uf)
    bu_im = jnp.einsum('gnc,bsgc->bsgn', bb_im, uf)
    a_re = jnp.broadcast_to(ab_re[None, None], (1, seq, SSM_GROUPS, SSM_STATE))
    a_im = jnp.broadcast_to(ab_im[None, None], (1, seq, SSM_GROUPS, SSM_STATE))
    _, _, s_re, s_im = lax.associative_scan(_ssm_combine, (a_re, a_im, bu_re, bu_im), axis=1)
    y = (jnp.einsum('gcn,bsgn->bsgc', c_re.astype(f32), s_re)
         - jnp.einsum('gcn,bsgn->bsgc', c_im.astype(f32), s_im))
    y = y + d_skip.astype(f32).reshape(SSM_GROUPS, SSM_GROUP) * uf
    return y.reshape(bsz, seq, SSM_WIDTH)


def causal_short_conv(v, w, b):
    seq = v.shape[1]
    vp = jnp.pad(v, ((0, 0), (CONV_K - 1, 0), (0, 0)))
    out = b
    for k in range(CONV_K):
        out = out + w[k] * vp[:, k:k + seq]
    return out


def mixer_block(xn, w_in, b_in, lam_re, lam_im, log_dt, b_re, b_im, c_re, c_im, d_skip,
                w_glu_a, w_glu_b, conv_w, conv_b, w_conv_out, w_out):
    proj = jnp.einsum('bsd,de->bse', xn, w_in) + b_in
    u_ssm, c_bgate, c_cgate, c_val, g_ssm, g_conv = jnp.split(proj, SPLIT_POINTS, axis=-1)
    y_ssm = s5_branch(u_ssm, lam_re, lam_im, log_dt, b_re, b_im, c_re, c_im, d_skip).astype(xn.dtype)
    z = jax.nn.gelu(y_ssm)
    y_a = jnp.einsum('bse,ed->bsd', z, w_glu_a) * jax.nn.sigmoid(jnp.einsum('bse,ed->bsd', z, w_glu_b))
    y_b = jnp.einsum('bse,ed->bsd', c_bgate * causal_short_conv(c_cgate * c_val, conv_w, conv_b), w_conv_out)
    merged = jax.nn.sigmoid(g_ssm) * y_a + jax.nn.sigmoid(g_conv) * y_b
    return jnp.einsum('bsd,de->bse', merged, w_out)


def squared_relu_mlp(xn, w_ff1, w_ff2):
    h = jax.nn.relu(jnp.einsum('bsd,df->bsf', xn, w_ff1))
    return jnp.einsum('bsf,fd->bsd', h * h, w_ff2)


def setup_inputs(seed: int = 0) -> dict:
    key = jax.random.key(seed)
    ks = jax.random.split(key, 24)
    L, D, G, N, C = DEPTH, D_MODEL, SSM_GROUPS, SSM_STATE, SSM_GROUP
    nrm = jax.random.normal
    x = nrm(ks[0], (BATCH, SEQ, D), jnp.float32)
    norm_mix_g = 1.0 + 0.02 * nrm(ks[1], (L, D), jnp.float32)
    w_in = nrm(ks[2], (L, D, IN_COLS), jnp.float32) * D ** -0.5
    b_in = 0.02 * nrm(ks[3], (L, IN_COLS), jnp.float32)
    n_idx = jnp.arange(N, dtype=jnp.float32)
    lam_re = -0.5 + 0.01 * nrm(ks[4], (L, G, N), jnp.float32)
    lam_im = math.pi * n_idx[None, None, :] + 0.01 * nrm(ks[5], (L, G, N), jnp.float32)
    log_dt = jax.random.uniform(ks[6], (L, G), jnp.float32, math.log(DT_MIN), math.log(DT_MAX))
    ssm_b_re = nrm(ks[7], (L, G, N, C), jnp.float32) * (2.0 * C) ** -0.5
    ssm_b_im = nrm(ks[8], (L, G, N, C), jnp.float32) * (2.0 * C) ** -0.5
    ssm_c_re = nrm(ks[9], (L, G, C, N), jnp.float32) * (2.0 * N) ** -0.5
    ssm_c_im = nrm(ks[10], (L, G, C, N), jnp.float32) * (2.0 * N) ** -0.5
    ssm_d = 1.0 + 0.1 * nrm(ks[11], (L, SSM_WIDTH), jnp.float32)
    w_glu_a = nrm(ks[12], (L, SSM_WIDTH, D), jnp.float32) * SSM_WIDTH ** -0.5
    w_glu_b = nrm(ks[13], (L, SSM_WIDTH, D), jnp.float32) * SSM_WIDTH ** -0.5
    conv_w = nrm(ks[14], (L, CONV_K, CONV_WIDTH), jnp.float32) * CONV_K ** -0.5
    conv_b = 0.02 * nrm(ks[15], (L, CONV_WIDTH), jnp.float32)
    w_conv_out = nrm(ks[16], (L, CONV_WIDTH, D), jnp.float32) * CONV_WIDTH ** -0.5
    w_out = nrm(ks[17], (L, D, D), jnp.float32) * D ** -0.5
    norm_mlp_g = 1.0 + 0.02 * nrm(ks[18], (L, D), jnp.float32)
    w_ff1 = nrm(ks[19], (L, D, D_FF), jnp.float32) * D ** -0.5
    w_ff2 = nrm(ks[20], (L, D_FF, D), jnp.float32) * D_FF ** -0.5
    norm_final_g = 1.0 + 0.02 * nrm(ks[21], (D,), jnp.float32)
    return {"x": x, "norm_mix_g": norm_mix_g, "w_in": w_in, "b_in": b_in,
            "lam_re": lam_re, "lam_im": lam_im, "log_dt": log_dt,
            "ssm_b_re": ssm_b_re, "ssm_b_im": ssm_b_im, "ssm_c_re": ssm_c_re, "ssm_c_im": ssm_c_im,
            "ssm_d": ssm_d, "w_glu_a": w_glu_a, "w_glu_b": w_glu_b,
            "conv_w": conv_w, "conv_b": conv_b, "w_conv_out": w_conv_out, "w_out": w_out,
            "norm_mlp_g": norm_mlp_g, "w_ff1": w_ff1, "w_ff2": w_ff2, "norm_final_g": norm_final_g}


def reference(x, norm_mix_g, w_in, b_in, lam_re, lam_im, log_dt, ssm_b_re, ssm_b_im, ssm_c_re, ssm_c_im,
              ssm_d, w_glu_a, w_glu_b, conv_w, conv_b, w_conv_out, w_out, norm_mlp_g, w_ff1, w_ff2,
              norm_final_g):
    h = x
    for l in range(DEPTH):
        xn = rmsnorm(h, norm_mix_g[l])
        h = h + mixer_block(xn, w_in[l], b_in[l], lam_re[l], lam_im[l], log_dt[l],
                            ssm_b_re[l], ssm_b_im[l], ssm_c_re[l], ssm_c_im[l], ssm_d[l],
                            w_glu_a[l], w_glu_b[l], conv_w[l], conv_b[l], w_conv_out[l], w_out[l])
        xn = rmsnorm(h, norm_mlp_g[l])
        h = h + squared_relu_mlp(xn, w_ff1[l], w_ff2[l])
    return rmsnorm(h, norm_final_g)
```

```python
import functools
import math

import jax
import jax.numpy as jnp
from jax import lax
from jax.experimental import pallas as pl
from jax.experimental.pallas import tpu as pltpu

LANES = 128
SUBLANES = 8
VMEM_LIMIT_BYTES = 56 * 1024 * 1024

D_MODEL = 1024
SSM_GROUP = 16
SSM_WIDTH = D_MODEL // 2
SSM_GROUPS = SSM_WIDTH // SSM_GROUP
SSM_STATE = 64
CONV_K = 3
D_FF = 4 * D_MODEL
NORM_EPS = 1e-6

BATCH_TILE = SUBLANES
SEQ_TILE = 32
MIX_ROWS = BATCH_TILE * SEQ_TILE
MLP_ROWS = 512
FF_CHUNK = 1024

GROUPS_PER_BLOCK = LANES // SSM_GROUP
N_BLOCKS = SSM_GROUPS // GROUPS_PER_BLOCK
BLOCK_STATE = GROUPS_PER_BLOCK * SSM_STATE
STATE_LANES = SSM_GROUPS * SSM_STATE
SCAN_LANES = 512

_O_U = 0
_O_CB = _O_U + SSM_WIDTH
_O_CC = _O_CB + D_MODEL
_O_CV = _O_CC + D_MODEL
_O_GS = _O_CV + D_MODEL
_O_GC = _O_GS + D_MODEL
IN_COLS = _O_GC + D_MODEL


def _rmsnorm(x, g):
    var = jnp.mean(x * x, axis=-1, keepdims=True)
    return x * lax.rsqrt(var + NORM_EPS) * g


def _dot(a, b):
    return jnp.dot(a, b, preferred_element_type=jnp.float32)


def _mixer_kernel(x_ref, gmix_ref, win_ref, bin_ref, are_ref, aim_ref, bw_ref, cw_ref, dskip_ref,
                  wglu_ref, convw_ref, convb_ref, wco_ref, wout_ref,
                  o_ref,
                  xn_slab, xn16, sr_ref, si_ref, csr_ref, csi_ref, p_ref, m_slab):
    j = pl.program_id(1)
    n_slab = D_MODEL // LANES
    halo = (CONV_K - 1) * SUBLANES

    @pl.when(j == 0)
    def _():
        csr_ref[...] = jnp.zeros_like(csr_ref)
        csi_ref[...] = jnp.zeros_like(csi_ref)
        p_ref[0:halo, :] = jnp.zeros((halo, D_MODEL), jnp.float32)

    g = gmix_ref[...]
    for b in range(BATCH_TILE):
        xnb = _rmsnorm(x_ref[b], g)
        for k in range(n_slab):
            xn_slab[k, pl.ds(b, SEQ_TILE, stride=SUBLANES), :] = xnb[:, k * LANES:(k + 1) * LANES]
    for k in range(n_slab):
        xn16[:, k * LANES:(k + 1) * LANES] = xn_slab[k].astype(jnp.bfloat16)

    def proj(off, width):
        return _dot(xn16[...], win_ref[:, off:off + width]) + bin_ref[:, off:off + width]

    u = proj(_O_U, SSM_WIDTH)
    for q in range(N_BLOCKS):
        bu = _dot(u[:, q * LANES:(q + 1) * LANES].astype(jnp.bfloat16), bw_ref[q])
        sr_ref[:, q * BLOCK_STATE:(q + 1) * BLOCK_STATE] = bu[:, :BLOCK_STATE]
        si_ref[:, q * BLOCK_STATE:(q + 1) * BLOCK_STATE] = bu[:, BLOCK_STATE:]

    for c in range(STATE_LANES // SCAN_LANES):
        sl = slice(c * SCAN_LANES, (c + 1) * SCAN_LANES)
        ar = jnp.broadcast_to(are_ref[:, sl], (SUBLANES, SCAN_LANES))
        ai = jnp.broadcast_to(aim_ref[:, sl], (SUBLANES, SCAN_LANES))

        def step(t, carry, sl=sl, ar=ar, ai=ai):
            cr, ci = carry
            r0 = pl.multiple_of(t * SUBLANES, SUBLANES)
            nr = ar * cr - ai * ci + sr_ref[pl.ds(r0, SUBLANES), sl]
            ni = ar * ci + ai * cr + si_ref[pl.ds(r0, SUBLANES), sl]
            sr_ref[pl.ds(r0, SUBLANES), sl] = nr
            si_ref[pl.ds(r0, SUBLANES), sl] = ni
            return nr, ni

        cr, ci = lax.fori_loop(0, SEQ_TILE, step, (csr_ref[:, sl], csi_ref[:, sl]), unroll=4)
        csr_ref[:, sl] = cr
        csi_ref[:, sl] = ci

    ys = []
    for q in range(N_BLOCKS):
        sq = jnp.concatenate([sr_ref[:, q * BLOCK_STATE:(q + 1) * BLOCK_STATE],
                              si_ref[:, q * BLOCK_STATE:(q + 1) * BLOCK_STATE]], axis=1)
        ys.append(_dot(sq.astype(jnp.bfloat16), cw_ref[q]))
    y_ssm = jnp.concatenate(ys, axis=1) + dskip_ref[...] * u
    z = jax.nn.gelu(y_ssm, approximate=True).astype(jnp.bfloat16)
    ab = _dot(z, wglu_ref[...])
    y_a = ab[:, :D_MODEL] * jax.nn.sigmoid(ab[:, D_MODEL:])
    merged = jax.nn.sigmoid(proj(_O_GS, D_MODEL)) * y_a

    p_ref[halo:halo + MIX_ROWS, :] = proj(_O_CC, D_MODEL) * proj(_O_CV, D_MODEL)
    conv = convb_ref[...]
    for k in range(CONV_K):
        conv = conv + convw_ref[k:k + 1, :] * p_ref[k * SUBLANES:k * SUBLANES + MIX_ROWS, :]
    p_ref[0:halo, :] = p_ref[MIX_ROWS:MIX_ROWS + halo, :]
    y_b = _dot((proj(_O_CB, D_MODEL) * conv).astype(jnp.bfloat16), wco_ref[...])
    merged = merged + jax.nn.sigmoid(proj(_O_GC, D_MODEL)) * y_b

    mix = _dot(merged.astype(jnp.bfloat16), wout_ref[...])
    for k in range(n_slab):
        m_slab[k] = mix[:, k * LANES:(k + 1) * LANES]
    for b in range(BATCH_TILE):
        mb = jnp.concatenate(
            [m_slab[k, pl.ds(b, SEQ_TILE, stride=SUBLANES), :] for k in range(n_slab)], axis=1)
        o_ref[b] = x_ref[b] + mb


def _resident(shape):
    return pl.BlockSpec(shape, lambda *_: (0,) * len(shape), pipeline_mode=pl.Buffered(1))


def _mixer(x, gmix, win, b_in, a_re, a_im, bw, cw, dskip, wglu, convw, convb, wco, wout):
    batch, seq, d = x.shape
    assert d == D_MODEL and batch % BATCH_TILE == 0 and seq % SEQ_TILE == 0
    tile = pl.BlockSpec((BATCH_TILE, SEQ_TILE, D_MODEL), lambda i, j: (i, j, 0))
    weights = (gmix, win, b_in, a_re, a_im, bw, cw, dskip, wglu, convw, convb, wco, wout)
    return pl.pallas_call(
        _mixer_kernel,
        out_shape=jax.ShapeDtypeStruct(x.shape, jnp.float32),
        grid=(batch // BATCH_TILE, seq // SEQ_TILE),
        in_specs=[tile] + [_resident(w.shape) for w in weights],
        out_specs=tile,
        scratch_shapes=[
            pltpu.VMEM((D_MODEL // LANES, MIX_ROWS, LANES), jnp.float32),
            pltpu.VMEM((MIX_ROWS, D_MODEL), jnp.bfloat16),
            pltpu.VMEM((MIX_ROWS, STATE_LANES), jnp.float32),
            pltpu.VMEM((MIX_ROWS, STATE_LANES), jnp.float32),
            pltpu.VMEM((SUBLANES, STATE_LANES), jnp.float32),
            pltpu.VMEM((SUBLANES, STATE_LANES), jnp.float32),
            pltpu.VMEM((MIX_ROWS + (CONV_K - 1) * SUBLANES, D_MODEL), jnp.float32),
            pltpu.VMEM((D_MODEL // LANES, MIX_ROWS, LANES), jnp.float32),
        ],
        compiler_params=pltpu.CompilerParams(
            dimension_semantics=("arbitrary", "arbitrary"),
            vmem_limit_bytes=VMEM_LIMIT_BYTES),
        name="mixer",
    )(x, *weights)


def _mlp_kernel(h_ref, gmlp_ref, w1_ref, w2_ref, gfin_ref, o_ref):
    h = h_ref[...]
    hn = _rmsnorm(h, gmlp_ref[...]).astype(jnp.bfloat16)
    acc = h
    for c in range(D_FF // FF_CHUNK):
        a = jnp.maximum(_dot(hn, w1_ref[:, c * FF_CHUNK:(c + 1) * FF_CHUNK]), 0.0)
        acc = acc + _dot((a * a).astype(jnp.bfloat16), w2_ref[c * FF_CHUNK:(c + 1) * FF_CHUNK, :])
    o_ref[...] = _rmsnorm(acc, gfin_ref[...])


def _mlp(h, gmlp, w1, w2, gfin):
    rows, d = h.shape
    assert d == D_MODEL and rows % MLP_ROWS == 0
    tile = pl.BlockSpec((MLP_ROWS, D_MODEL), lambda i: (i, 0))
    weights = (gmlp, w1, w2, gfin)
    in_specs = [tile, _resident(gmlp.shape), _resident(w1.shape), _resident(w2.shape), _resident(gfin.shape)]
    return pl.pallas_call(
        _mlp_kernel,
        out_shape=jax.ShapeDtypeStruct(h.shape, jnp.float32),
        grid=(rows // MLP_ROWS,),
        in_specs=in_specs,
        out_specs=tile,
        compiler_params=pltpu.CompilerParams(
            dimension_semantics=("arbitrary",),
            vmem_limit_bytes=VMEM_LIMIT_BYTES),
        name="mlp",
    )(h, *weights)


def _discretize(lam_re, lam_im, log_dt, b_re, b_im):
    dt = jnp.exp(log_dt)[:, None]
    mag = jnp.exp(lam_re * dt)
    ab_re = mag * jnp.cos(lam_im * dt)
    ab_im = mag * jnp.sin(lam_im * dt)
    er = ab_re - 1.0
    ei = ab_im
    den = lam_re * lam_re + lam_im * lam_im
    q_re = (er * lam_re + ei * lam_im) / den
    q_im = (ei * lam_re - er * lam_im) / den
    bb_re = q_re[..., None] * b_re - q_im[..., None] * b_im
    bb_im = q_re[..., None] * b_im + q_im[..., None] * b_re
    return ab_re, ab_im, bb_re, bb_im


def _block_diag_in(bb):
    b4 = bb.reshape(N_BLOCKS, GROUPS_PER_BLOCK, SSM_STATE, SSM_GROUP)
    eye = jnp.eye(GROUPS_PER_BLOCK, dtype=bb.dtype)
    return jnp.einsum('qgnc,gh->qgchn', b4, eye).reshape(N_BLOCKS, LANES, BLOCK_STATE)


def _block_diag_out(cc):
    c4 = cc.reshape(N_BLOCKS, GROUPS_PER_BLOCK, SSM_GROUP, SSM_STATE)
    eye = jnp.eye(GROUPS_PER_BLOCK, dtype=cc.dtype)
    return jnp.einsum('qgcn,gh->qgnhc', c4, eye).reshape(N_BLOCKS, BLOCK_STATE, LANES)


def _layer(h, norm_mix_g, w_in, b_in, lam_re, lam_im, log_dt, ssm_b_re, ssm_b_im, ssm_c_re, ssm_c_im,
           ssm_d, w_glu_a, w_glu_b, conv_w, conv_b, w_conv_out, w_out):
    bf16 = jnp.bfloat16
    ab_re, ab_im, bb_re, bb_im = _discretize(lam_re, lam_im, log_dt, ssm_b_re, ssm_b_im)
    bw = jnp.concatenate([_block_diag_in(bb_re), _block_diag_in(bb_im)], axis=2).astype(bf16)
    cw = jnp.concatenate([_block_diag_out(ssm_c_re), -_block_diag_out(ssm_c_im)], axis=1).astype(bf16)
    return _mixer(
        h, norm_mix_g.reshape(1, D_MODEL), w_in.astype(bf16), b_in.reshape(1, IN_COLS),
        ab_re.reshape(1, STATE_LANES), ab_im.reshape(1, STATE_LANES), bw, cw,
        ssm_d.reshape(1, SSM_WIDTH), jnp.concatenate([w_glu_a, w_glu_b], axis=1).astype(bf16),
        conv_w, conv_b.reshape(1, D_MODEL), w_conv_out.astype(bf16), w_out.astype(bf16))


def kernel(x, norm_mix_g, w_in, b_in, lam_re, lam_im, log_dt, ssm_b_re, ssm_b_im, ssm_c_re, ssm_c_im,
           ssm_d, w_glu_a, w_glu_b, conv_w, conv_b, w_conv_out, w_out, norm_mlp_g, w_ff1, w_ff2,
           norm_final_g):
    depth = norm_mix_g.shape[0]
    assert depth == 1, "the MLP call applies the final norm, so it must be the last layer"
    batch, seq, d = x.shape
    l = 0
    h = _layer(x, norm_mix_g[l], w_in[l], b_in[l], lam_re[l], lam_im[l], log_dt[l],
               ssm_b_re[l], ssm_b_im[l], ssm_c_re[l], ssm_c_im[l], ssm_d[l],
               w_glu_a[l], w_glu_b[l], conv_w[l], conv_b[l], w_conv_out[l], w_out[l])
    out = _mlp(h.reshape(batch * seq, d), norm_mlp_g[l].reshape(1, d), w_ff1[l].astype(jnp.bfloat16),
               w_ff2[l].astype(jnp.bfloat16), norm_final_g.reshape(1, d))
    return out.reshape(batch, seq, d)
```

```python
import jax
import jax.numpy as jnp
from jax import lax
from jax.experimental import pallas as pl
from jax.experimental.pallas import tpu as pltpu

LANES = 128
SUBLANES = 8
VMEM_LIMIT_BYTES = 56 * 1024 * 1024

D_MODEL = 1024
SSM_GROUP = 16
SSM_WIDTH = D_MODEL // 2
SSM_GROUPS = SSM_WIDTH // SSM_GROUP
SSM_STATE = 64
CONV_K = 3
D_FF = 4 * D_MODEL
NORM_EPS = 1e-6

BATCH_TILE = SUBLANES
SEQ_TILE = 64
MIX_ROWS = BATCH_TILE * SEQ_TILE
MLP_ROWS = 1024
FF_CHUNK = 1024

GROUPS_PER_BLOCK = LANES // SSM_GROUP
N_BLOCKS = SSM_GROUPS // GROUPS_PER_BLOCK
BLOCK_STATE = GROUPS_PER_BLOCK * SSM_STATE
STATE_LANES = SSM_GROUPS * SSM_STATE
SCAN_LANES = 512

_O_U = 0
_O_CB = _O_U + SSM_WIDTH
_O_CC = _O_CB + D_MODEL
_O_CV = _O_CC + D_MODEL
_O_GS = _O_CV + D_MODEL
_O_GC = _O_GS + D_MODEL
IN_COLS = _O_GC + D_MODEL


def _rmsnorm(x, g):
    var = jnp.mean(x * x, axis=-1, keepdims=True)
    return x * lax.rsqrt(var + NORM_EPS) * g


def _dot(a, b):
    return jnp.dot(a, b, preferred_element_type=jnp.float32)


def _mixer_kernel(x_ref, gmix_ref, win_ref, bin_ref, are_ref, aim_ref, bw_ref, cw_ref, dskip_ref,
                  wglu_ref, convw_ref, convb_ref, wco_ref, wout_ref,
                  o_ref,
                  xn_slab, xn16, sr_ref, si_ref, csr_ref, csi_ref, p_ref, m_slab):
    j = pl.program_id(1)
    n_slab = D_MODEL // LANES
    halo = (CONV_K - 1) * SUBLANES

    @pl.when(j == 0)
    def _():
        csr_ref[...] = jnp.zeros_like(csr_ref)
        csi_ref[...] = jnp.zeros_like(csi_ref)
        p_ref[0:halo, :] = jnp.zeros((halo, D_MODEL), jnp.float32)

    g = gmix_ref[...]
    for b in range(BATCH_TILE):
        xnb = _rmsnorm(x_ref[b], g)
        for k in range(n_slab):
            xn_slab[k, pl.ds(b, SEQ_TILE, stride=SUBLANES), :] = xnb[:, k * LANES:(k + 1) * LANES]
    for k in range(n_slab):
        xn16[:, k * LANES:(k + 1) * LANES] = xn_slab[k].astype(jnp.bfloat16)

    def proj(off, width):
        return _dot(xn16[...], win_ref[:, off:off + width]) + bin_ref[:, off:off + width]

    u = proj(_O_U, SSM_WIDTH)
    for q in range(N_BLOCKS):
        bu = _dot(u[:, q * LANES:(q + 1) * LANES].astype(jnp.bfloat16), bw_ref[q])
        sr_ref[:, q * BLOCK_STATE:(q + 1) * BLOCK_STATE] = bu[:, :BLOCK_STATE]
        si_ref[:, q * BLOCK_STATE:(q + 1) * BLOCK_STATE] = bu[:, BLOCK_STATE:]

    for c in range(STATE_LANES // SCAN_LANES):
        sl = slice(c * SCAN_LANES, (c + 1) * SCAN_LANES)
        ar = jnp.broadcast_to(are_ref[:, sl], (SUBLANES, SCAN_LANES))
        ai = jnp.broadcast_to(aim_ref[:, sl], (SUBLANES, SCAN_LANES))

        cr, ci = csr_ref[:, sl], csi_ref[:, sl]
        for t in range(SEQ_TILE):
            rows = slice(t * SUBLANES, (t + 1) * SUBLANES)
            nr = ar * cr - ai * ci + sr_ref[rows, sl]
            ni = ar * ci + ai * cr + si_ref[rows, sl]
            sr_ref[rows, sl] = nr
            si_ref[rows, sl] = ni
            cr, ci = nr, ni
        csr_ref[:, sl] = cr
        csi_ref[:, sl] = ci

    ys = []
    for q in range(N_BLOCKS):
        sq = jnp.concatenate([sr_ref[:, q * BLOCK_STATE:(q + 1) * BLOCK_STATE],
                              si_ref[:, q * BLOCK_STATE:(q + 1) * BLOCK_STATE]], axis=1)
        ys.append(_dot(sq.astype(jnp.bfloat16), cw_ref[q]))
    y_ssm = jnp.concatenate(ys, axis=1) + dskip_ref[...] * u
    z = jax.nn.gelu(y_ssm, approximate=True).astype(jnp.bfloat16)
    ab = _dot(z, wglu_ref[...])
    y_a = ab[:, :D_MODEL] * jax.nn.sigmoid(ab[:, D_MODEL:])
    merged = jax.nn.sigmoid(proj(_O_GS, D_MODEL)) * y_a

    p_ref[halo:halo + MIX_ROWS, :] = proj(_O_CC, D_MODEL) * proj(_O_CV, D_MODEL)
    conv = convb_ref[...]
    for k in range(CONV_K):
        conv = conv + convw_ref[k:k + 1, :] * p_ref[k * SUBLANES:k * SUBLANES + MIX_ROWS, :]
    p_ref[0:halo, :] = p_ref[MIX_ROWS:MIX_ROWS + halo, :]
    y_b = _dot((proj(_O_CB, D_MODEL) * conv).astype(jnp.bfloat16), wco_ref[...])
    merged = merged + jax.nn.sigmoid(proj(_O_GC, D_MODEL)) * y_b

    mix = _dot(merged.astype(jnp.bfloat16), wout_ref[...])
    for k in range(n_slab):
        m_slab[k] = mix[:, k * LANES:(k + 1) * LANES]
    for b in range(BATCH_TILE):
        mb = jnp.concatenate(
            [m_slab[k, pl.ds(b, SEQ_TILE, stride=SUBLANES), :] for k in range(n_slab)], axis=1)
        o_ref[b] = x_ref[b] + mb


def _resident(shape):
    return pl.BlockSpec(shape, lambda *_: (0,) * len(shape), pipeline_mode=pl.Buffered(1))


def _mixer(x, gmix, win, b_in, a_re, a_im, bw, cw, dskip, wglu, convw, convb, wco, wout):
    batch, seq, d = x.shape
    assert d == D_MODEL and batch % BATCH_TILE == 0 and seq % SEQ_TILE == 0
    tile = pl.BlockSpec((BATCH_TILE, SEQ_TILE, D_MODEL), lambda i, j: (i, j, 0))
    weights = (gmix, win, b_in, a_re, a_im, bw, cw, dskip, wglu, convw, convb, wco, wout)
    return pl.pallas_call(
        _mixer_kernel,
        out_shape=jax.ShapeDtypeStruct(x.shape, jnp.float32),
        grid=(batch // BATCH_TILE, seq // SEQ_TILE),
        in_specs=[tile] + [_resident(w.shape) for w in weights],
        out_specs=tile,
        scratch_shapes=[
            pltpu.VMEM((D_MODEL // LANES, MIX_ROWS, LANES), jnp.float32),
            pltpu.VMEM((MIX_ROWS, D_MODEL), jnp.bfloat16),
            pltpu.VMEM((MIX_ROWS, STATE_LANES), jnp.float32),
            pltpu.VMEM((MIX_ROWS, STATE_LANES), jnp.float32),
            pltpu.VMEM((SUBLANES, STATE_LANES), jnp.float32),
            pltpu.VMEM((SUBLANES, STATE_LANES), jnp.float32),
            pltpu.VMEM((MIX_ROWS + (CONV_K - 1) * SUBLANES, D_MODEL), jnp.float32),
            pltpu.VMEM((D_MODEL // LANES, MIX_ROWS, LANES), jnp.float32),
        ],
        compiler_params=pltpu.CompilerParams(
            dimension_semantics=("arbitrary", "arbitrary"),
            vmem_limit_bytes=VMEM_LIMIT_BYTES),
        name="mixer",
    )(x, *weights)


def _mlp_kernel(h_ref, gmlp_ref, w1_ref, w2_ref, gfin_ref, o_ref):
    h = h_ref[...]
    hn = _rmsnorm(h, gmlp_ref[...]).astype(jnp.bfloat16)
    acc = h
    for c in range(D_FF // FF_CHUNK):
        a = jnp.maximum(_dot(hn, w1_ref[:, c * FF_CHUNK:(c + 1) * FF_CHUNK]), 0.0)
        acc = acc + _dot((a * a).astype(jnp.bfloat16), w2_ref[c * FF_CHUNK:(c + 1) * FF_CHUNK, :])
    o_ref[...] = _rmsnorm(acc, gfin_ref[...])


def _mlp(h, gmlp, w1, w2, gfin):
    rows, d = h.shape
    assert d == D_MODEL and rows % MLP_ROWS == 0
    tile = pl.BlockSpec((MLP_ROWS, D_MODEL), lambda i: (i, 0))
    weights = (gmlp, w1, w2, gfin)
    in_specs = [tile, _resident(gmlp.shape), _resident(w1.shape), _resident(w2.shape), _resident(gfin.shape)]
    return pl.pallas_call(
        _mlp_kernel,
        out_shape=jax.ShapeDtypeStruct(h.shape, jnp.float32),
        grid=(rows // MLP_ROWS,),
        in_specs=in_specs,
        out_specs=tile,
        compiler_params=pltpu.CompilerParams(
            dimension_semantics=("arbitrary",),
            vmem_limit_bytes=VMEM_LIMIT_BYTES),
        name="mlp",
    )(h, *weights)


def _discretize(lam_re, lam_im, log_dt, b_re, b_im):
    dt = jnp.exp(log_dt)[:, None]
    mag = jnp.exp(lam_re * dt)
    ab_re = mag * jnp.cos(lam_im * dt)
    ab_im = mag * jnp.sin(lam_im * dt)
    er = ab_re - 1.0
    ei = ab_im
    den = lam_re * lam_re + lam_im * lam_im
    q_re = (er * lam_re + ei * lam_im) / den
    q_im = (ei * lam_re - er * lam_im) / den
    bb_re = q_re[..., None] * b_re - q_im[..., None] * b_im
    bb_im = q_re[..., None] * b_im + q_im[..., None] * b_re
    return ab_re, ab_im, bb_re, bb_im


def _block_diag_in(bb):
    b4 = bb.reshape(N_BLOCKS, GROUPS_PER_BLOCK, SSM_STATE, SSM_GROUP)
    eye = jnp.eye(GROUPS_PER_BLOCK, dtype=bb.dtype)
    return jnp.einsum('qgnc,gh->qgchn', b4, eye).reshape(N_BLOCKS, LANES, BLOCK_STATE)


def _block_diag_out(cc):
    c4 = cc.reshape(N_BLOCKS, GROUPS_PER_BLOCK, SSM_GROUP, SSM_STATE)
    eye = jnp.eye(GROUPS_PER_BLOCK, dtype=cc.dtype)
    return jnp.einsum('qgcn,gh->qgnhc', c4, eye).reshape(N_BLOCKS, BLOCK_STATE, LANES)


def _layer(h, norm_mix_g, w_in, b_in, lam_re, lam_im, log_dt, ssm_b_re, ssm_b_im, ssm_c_re, ssm_c_im,
           ssm_d, w_glu_a, w_glu_b, conv_w, conv_b, w_conv_out, w_out):
    bf16 = jnp.bfloat16
    ab_re, ab_im, bb_re, bb_im = _discretize(lam_re, lam_im, log_dt, ssm_b_re, ssm_b_im)
    bw = jnp.concatenate([_block_diag_in(bb_re), _block_diag_in(bb_im)], axis=2).astype(bf16)
    cw = jnp.concatenate([_block_diag_out(ssm_c_re), -_block_diag_out(ssm_c_im)], axis=1).astype(bf16)
    return _mixer(
        h, norm_mix_g.reshape(1, D_MODEL), w_in.astype(bf16), b_in.reshape(1, IN_COLS),
        ab_re.reshape(1, STATE_LANES), ab_im.reshape(1, STATE_LANES), bw, cw,
        ssm_d.reshape(1, SSM_WIDTH), jnp.concatenate([w_glu_a, w_glu_b], axis=1).astype(bf16),
        conv_w, conv_b.reshape(1, D_MODEL), w_conv_out.astype(bf16), w_out.astype(bf16))


def kernel(x, norm_mix_g, w_in, b_in, lam_re, lam_im, log_dt, ssm_b_re, ssm_b_im, ssm_c_re, ssm_c_im,
           ssm_d, w_glu_a, w_glu_b, conv_w, conv_b, w_conv_out, w_out, norm_mlp_g, w_ff1, w_ff2,
           norm_final_g):
    depth = norm_mix_g.shape[0]
    assert depth == 1, "the MLP call applies the final norm, so it must be the last layer"
    batch, seq, d = x.shape
    l = 0
    h = _layer(x, norm_mix_g[l], w_in[l], b_in[l], lam_re[l], lam_im[l], log_dt[l],
               ssm_b_re[l], ssm_b_im[l], ssm_c_re[l], ssm_c_im[l], ssm_d[l],
               w_glu_a[l], w_glu_b[l], conv_w[l], conv_b[l], w_conv_out[l], w_out[l])
    out = _mlp(h.reshape(batch * seq, d), norm_mlp_g[l].reshape(1, d), w_ff1[l].astype(jnp.bfloat16),
               w_ff2[l].astype(jnp.bfloat16), norm_final_g.reshape(1, d))
    return out.reshape(batch, seq, d)
```

```python
import jax
import jax.numpy as jnp
from jax import lax
from jax.experimental import pallas as pl
from jax.experimental.pallas import tpu as pltpu

LANES = 128
SUBLANES = 8
VMEM_LIMIT_BYTES = 56 * 1024 * 1024

D_MODEL = 1024
SSM_GROUP = 16
SSM_WIDTH = D_MODEL // 2
SSM_GROUPS = SSM_WIDTH // SSM_GROUP
SSM_STATE = 64
CONV_K = 3
D_FF = 4 * D_MODEL
NORM_EPS = 1e-6

BATCH_TILE = SUBLANES
SEQ_TILE = 64
MIX_ROWS = BATCH_TILE * SEQ_TILE
SSM_CHUNK = 4
N_CHUNKS = SEQ_TILE // SSM_CHUNK
CHUNK_ROWS = N_CHUNKS * SUBLANES
MLP_ROWS = 1024
FF_CHUNK = 1024

GROUPS_PER_BLOCK = LANES // SSM_GROUP
N_BLOCKS = SSM_GROUPS // GROUPS_PER_BLOCK
BLOCK_STATE = GROUPS_PER_BLOCK * SSM_STATE
STATE_LANES = SSM_GROUPS * SSM_STATE
SCAN_LANES = 512

_O_U = 0
_O_CB = _O_U + SSM_WIDTH
_O_CC = _O_CB + D_MODEL
_O_CV = _O_CC + D_MODEL
_O_GS = _O_CV + D_MODEL
_O_GC = _O_GS + D_MODEL
IN_COLS = _O_GC + D_MODEL


def _rmsnorm(x, g):
    var = jnp.mean(x * x, axis=-1, keepdims=True)
    return x * lax.rsqrt(var + NORM_EPS) * g


def _dot(a, b):
    return jnp.dot(a, b, preferred_element_type=jnp.float32)


def _mixer_kernel(x_ref, gmix_ref, win_ref, bin_ref, are_ref, aim_ref, bw_ref, cw_ref, tw_ref, dskip_ref,
                  wglu_ref, convw_ref, convb_ref, wco_ref, wout_ref,
                  o_ref,
                  xn_slab, xn16, sr_ref, si_ref, csr_ref, csi_ref, p_ref, m_slab):
    j = pl.program_id(1)
    n_slab = D_MODEL // LANES
    halo = (CONV_K - 1) * SUBLANES

    @pl.when(j == 0)
    def _():
        csr_ref[...] = jnp.zeros_like(csr_ref)
        csi_ref[...] = jnp.zeros_like(csi_ref)
        p_ref[0:halo, :] = jnp.zeros((halo, D_MODEL), jnp.float32)

    g = gmix_ref[...]
    for b in range(BATCH_TILE):
        xnb = _rmsnorm(x_ref[b], g)
        for k in range(n_slab):
            xn_slab[k, pl.ds(b, SEQ_TILE, stride=SUBLANES), :] = xnb[:, k * LANES:(k + 1) * LANES]
    for k in range(n_slab):
        xn16[:, k * LANES:(k + 1) * LANES] = xn_slab[k].astype(jnp.bfloat16)

    def proj(off, width):
        return _dot(xn16[...], win_ref[:, off:off + width]) + bin_ref[:, off:off + width]

    u = proj(_O_U, SSM_WIDTH)
    u4 = u.reshape(N_CHUNKS, SSM_CHUNK, SUBLANES, SSM_WIDTH)
    ub = []
    for q in range(N_BLOCKS):
        ub.append(jnp.concatenate(
            [u4[:, jj, :, q * LANES:(q + 1) * LANES].reshape(CHUNK_ROWS, LANES) for jj in range(SSM_CHUNK)],
            axis=1).astype(jnp.bfloat16))
        v = _dot(ub[q], bw_ref[q])
        sr_ref[:, q * BLOCK_STATE:(q + 1) * BLOCK_STATE] = v[:, :BLOCK_STATE]
        si_ref[:, q * BLOCK_STATE:(q + 1) * BLOCK_STATE] = v[:, BLOCK_STATE:]

    for c in range(STATE_LANES // SCAN_LANES):
        sl = slice(c * SCAN_LANES, (c + 1) * SCAN_LANES)
        ar = jnp.broadcast_to(are_ref[:, sl], (SUBLANES, SCAN_LANES))
        ai = jnp.broadcast_to(aim_ref[:, sl], (SUBLANES, SCAN_LANES))
        cr, ci = csr_ref[:, sl], csi_ref[:, sl]
        for k in range(N_CHUNKS):
            rows = slice(k * SUBLANES, (k + 1) * SUBLANES)
            vr, vi = sr_ref[rows, sl], si_ref[rows, sl]
            sr_ref[rows, sl] = cr
            si_ref[rows, sl] = ci
            cr, ci = ar * cr - ai * ci + vr, ar * ci + ai * cr + vi
        csr_ref[:, sl] = cr
        csi_ref[:, sl] = ci

    ys = []
    for q in range(N_BLOCKS):
        sq = jnp.concatenate([sr_ref[:, q * BLOCK_STATE:(q + 1) * BLOCK_STATE],
                              si_ref[:, q * BLOCK_STATE:(q + 1) * BLOCK_STATE]], axis=1)
        yb = _dot(sq.astype(jnp.bfloat16), cw_ref[q]) + _dot(ub[q], tw_ref[q])
        ys.append(jnp.concatenate(
            [yb[:, jj * LANES:(jj + 1) * LANES].reshape(N_CHUNKS, 1, SUBLANES, LANES)
             for jj in range(SSM_CHUNK)], axis=1).reshape(MIX_ROWS, LANES))
    y_ssm = jnp.concatenate(ys, axis=1) + dskip_ref[...] * u
    z = jax.nn.gelu(y_ssm, approximate=True).astype(jnp.bfloat16)
    ab = _dot(z, wglu_ref[...])
    y_a = ab[:, :D_MODEL] * jax.nn.sigmoid(ab[:, D_MODEL:])
    merged = jax.nn.sigmoid(proj(_O_GS, D_MODEL)) * y_a

    p_ref[halo:halo + MIX_ROWS, :] = proj(_O_CC, D_MODEL) * proj(_O_CV, D_MODEL)
    conv = convb_ref[...]
    for k in range(CONV_K):
        conv = conv + convw_ref[k:k + 1, :] * p_ref[k * SUBLANES:k * SUBLANES + MIX_ROWS, :]
    p_ref[0:halo, :] = p_ref[MIX_ROWS:MIX_ROWS + halo, :]
    y_b = _dot((proj(_O_CB, D_MODEL) * conv).astype(jnp.bfloat16), wco_ref[...])
    merged = merged + jax.nn.sigmoid(proj(_O_GC, D_MODEL)) * y_b

    mix = _dot(merged.astype(jnp.bfloat16), wout_ref[...])
    for k in range(n_slab):
        m_slab[k] = mix[:, k * LANES:(k + 1) * LANES]
    for b in range(BATCH_TILE):
        mb = jnp.concatenate(
            [m_slab[k, pl.ds(b, SEQ_TILE, stride=SUBLANES), :] for k in range(n_slab)], axis=1)
        o_ref[b] = x_ref[b] + mb


def _resident(shape):
    return pl.BlockSpec(shape, lambda *_: (0,) * len(shape), pipeline_mode=pl.Buffered(1))


def _mixer(x, gmix, win, b_in, a_re, a_im, bw, cw, tw, dskip, wglu, convw, convb, wco, wout):
    batch, seq, d = x.shape
    assert d == D_MODEL and batch % BATCH_TILE == 0 and seq % SEQ_TILE == 0
    tile = pl.BlockSpec((BATCH_TILE, SEQ_TILE, D_MODEL), lambda i, j: (i, j, 0))
    weights = (gmix, win, b_in, a_re, a_im, bw, cw, tw, dskip, wglu, convw, convb, wco, wout)
    return pl.pallas_call(
        _mixer_kernel,
        out_shape=jax.ShapeDtypeStruct(x.shape, jnp.float32),
        grid=(batch // BATCH_TILE, seq // SEQ_TILE),
        in_specs=[tile] + [_resident(w.shape) for w in weights],
        out_specs=tile,
        scratch_shapes=[
            pltpu.VMEM((D_MODEL // LANES, MIX_ROWS, LANES), jnp.float32),
            pltpu.VMEM((MIX_ROWS, D_MODEL), jnp.bfloat16),
            pltpu.VMEM((CHUNK_ROWS, STATE_LANES), jnp.float32),
            pltpu.VMEM((CHUNK_ROWS, STATE_LANES), jnp.float32),
            pltpu.VMEM((SUBLANES, STATE_LANES), jnp.float32),
            pltpu.VMEM((SUBLANES, STATE_LANES), jnp.float32),
            pltpu.VMEM((MIX_ROWS + (CONV_K - 1) * SUBLANES, D_MODEL), jnp.float32),
            pltpu.VMEM((D_MODEL // LANES, MIX_ROWS, LANES), jnp.float32),
        ],
        compiler_params=pltpu.CompilerParams(
            dimension_semantics=("arbitrary", "arbitrary"),
            vmem_limit_bytes=VMEM_LIMIT_BYTES),
        name="mixer",
    )(x, *weights)


def _mlp_kernel(h_ref, gmlp_ref, w1_ref, w2_ref, gfin_ref, o_ref):
    h = h_ref[...]
    hn = _rmsnorm(h, gmlp_ref[...]).astype(jnp.bfloat16)
    acc = h
    for c in range(D_FF // FF_CHUNK):
        a = jnp.maximum(_dot(hn, w1_ref[:, c * FF_CHUNK:(c + 1) * FF_CHUNK]), 0.0)
        acc = acc + _dot((a * a).astype(jnp.bfloat16), w2_ref[c * FF_CHUNK:(c + 1) * FF_CHUNK, :])
    o_ref[...] = _rmsnorm(acc, gfin_ref[...])


def _mlp(h, gmlp, w1, w2, gfin):
    rows, d = h.shape
    assert d == D_MODEL and rows % MLP_ROWS == 0
    tile = pl.BlockSpec((MLP_ROWS, D_MODEL), lambda i: (i, 0))
    weights = (gmlp, w1, w2, gfin)
    in_specs = [tile, _resident(gmlp.shape), _resident(w1.shape), _resident(w2.shape), _resident(gfin.shape)]
    return pl.pallas_call(
        _mlp_kernel,
        out_shape=jax.ShapeDtypeStruct(h.shape, jnp.float32),
        grid=(rows // MLP_ROWS,),
        in_specs=in_specs,
        out_specs=tile,
        compiler_params=pltpu.CompilerParams(
            dimension_semantics=("arbitrary",),
            vmem_limit_bytes=VMEM_LIMIT_BYTES),
        name="mlp",
    )(h, *weights)


def _discretize(lam_re, lam_im, log_dt, b_re, b_im):
    dt = jnp.exp(log_dt)[:, None]
    mag = jnp.exp(lam_re * dt)
    ab_re = mag * jnp.cos(lam_im * dt)
    ab_im = mag * jnp.sin(lam_im * dt)
    er = ab_re - 1.0
    ei = ab_im
    den = lam_re * lam_re + lam_im * lam_im
    q_re = (er * lam_re + ei * lam_im) / den
    q_im = (ei * lam_re - er * lam_im) / den
    bb_re = q_re[..., None] * b_re - q_im[..., None] * b_im
    bb_im = q_re[..., None] * b_im + q_im[..., None] * b_re
    return ab_re, ab_im, bb_re, bb_im


def _block_diag_in(bb):
    b4 = bb.reshape(N_BLOCKS, GROUPS_PER_BLOCK, SSM_STATE, SSM_GROUP)
    eye = jnp.eye(GROUPS_PER_BLOCK, dtype=bb.dtype)
    return jnp.einsum('qgnc,gh->qgchn', b4, eye).reshape(N_BLOCKS, LANES, BLOCK_STATE)


def _block_diag_out(cc):
    c4 = cc.reshape(N_BLOCKS, GROUPS_PER_BLOCK, SSM_GROUP, SSM_STATE)
    eye = jnp.eye(GROUPS_PER_BLOCK, dtype=cc.dtype)
    return jnp.einsum('qgcn,gh->qgnhc', c4, eye).reshape(N_BLOCKS, BLOCK_STATE, LANES)


def _block_diag_mix(kk):
    k4 = kk.reshape(N_BLOCKS, GROUPS_PER_BLOCK, SSM_GROUP, SSM_GROUP)
    eye = jnp.eye(GROUPS_PER_BLOCK, dtype=kk.dtype)
    return jnp.einsum('qgoi,gh->qgiho', k4, eye).reshape(N_BLOCKS, LANES, LANES)


def _chunk_weights(ab_re, ab_im, bb_re, bb_im, c_re, c_im):
    n = SSM_CHUNK
    pr, pi = [jnp.ones_like(ab_re)], [jnp.zeros_like(ab_im)]
    for _ in range(n):
        pr, pi = (pr + [pr[-1] * ab_re - pi[-1] * ab_im], pi + [pr[-1] * ab_im + pi[-1] * ab_re])

    def a_pow_b(m):
        return (pr[m][..., None] * bb_re - pi[m][..., None] * bb_im,
                pr[m][..., None] * bb_im + pi[m][..., None] * bb_re)

    def c_a_pow(m):
        return (c_re * pr[m][:, None, :] - c_im * pi[m][:, None, :],
                c_re * pi[m][:, None, :] + c_im * pr[m][:, None, :])

    def k_mat(m):
        car, cai = c_a_pow(m)
        hi = lax.Precision.HIGHEST
        return (jnp.einsum('gon,gni->goi', car, bb_re, precision=hi)
                - jnp.einsum('gon,gni->goi', cai, bb_im, precision=hi))

    bw = jnp.concatenate(
        [jnp.concatenate([_block_diag_in(a_pow_b(n - 1 - j)[part]) for j in range(n)], axis=1)
         for part in (0, 1)], axis=2)
    cw = jnp.concatenate(
        [jnp.concatenate([sign * _block_diag_out(c_a_pow(j + 1)[part]) for j in range(n)], axis=2)
         for part, sign in ((0, 1.0), (1, -1.0))], axis=1)
    ks = [_block_diag_mix(k_mat(m)) for m in range(n)]
    zero = jnp.zeros_like(ks[0])
    tw = jnp.concatenate(
        [jnp.concatenate([ks[j - i] if j >= i else zero for j in range(n)], axis=2) for i in range(n)],
        axis=1)
    return pr[n], pi[n], bw, cw, tw


def _layer(h, norm_mix_g, w_in, b_in, lam_re, lam_im, log_dt, ssm_b_re, ssm_b_im, ssm_c_re, ssm_c_im,
           ssm_d, w_glu_a, w_glu_b, conv_w, conv_b, w_conv_out, w_out):
    bf16 = jnp.bfloat16
    ab_re, ab_im, bb_re, bb_im = _discretize(lam_re, lam_im, log_dt, ssm_b_re, ssm_b_im)
    an_re, an_im, bw, cw, tw = _chunk_weights(ab_re, ab_im, bb_re, bb_im, ssm_c_re, ssm_c_im)
    return _mixer(
        h, norm_mix_g.reshape(1, D_MODEL), w_in.astype(bf16), b_in.reshape(1, IN_COLS),
        an_re.reshape(1, STATE_LANES), an_im.reshape(1, STATE_LANES),
        bw.astype(bf16), cw.astype(bf16), tw.astype(bf16),
        ssm_d.reshape(1, SSM_WIDTH), jnp.concatenate([w_glu_a, w_glu_b], axis=1).astype(bf16),
        conv_w, conv_b.reshape(1, D_MODEL), w_conv_out.astype(bf16), w_out.astype(bf16))


def kernel(x, norm_mix_g, w_in, b_in, lam_re, lam_im, log_dt, ssm_b_re, ssm_b_im, ssm_c_re, ssm_c_im,
           ssm_d, w_glu_a, w_glu_b, conv_w, conv_b, w_conv_out, w_out, norm_mlp_g, w_ff1, w_ff2,
           norm_final_g):
    depth = norm_mix_g.shape[0]
    assert depth == 1, "the MLP call applies the final norm, so it must be the last layer"
    batch, seq, d = x.shape
    l = 0
    h = _layer(x, norm_mix_g[l], w_in[l], b_in[l], lam_re[l], lam_im[l], log_dt[l],
               ssm_b_re[l], ssm_b_im[l], ssm_c_re[l], ssm_c_im[l], ssm_d[l],
               w_glu_a[l], w_glu_b[l], conv_w[l], conv_b[l], w_conv_out[l], w_out[l])
    out = _mlp(h.reshape(batch * seq, d), norm_mlp_g[l].reshape(1, d), w_ff1[l].astype(jnp.bfloat16),
               w_ff2[l].astype(jnp.bfloat16), norm_final_g.reshape(1, d))
    return out.reshape(batch, seq, d)
```

```python
import numpy as np
import jax
import jax.numpy as jnp
from jax import lax
from jax.experimental import pallas as pl
from jax.experimental.pallas import tpu as pltpu

LANES = 128
SUBLANES = 8
VMEM_LIMIT_BYTES = 56 * 1024 * 1024

D_MODEL = 1024
SSM_GROUP = 16
SSM_WIDTH = D_MODEL // 2
SSM_GROUPS = SSM_WIDTH // SSM_GROUP
SSM_STATE = 64
CONV_K = 3
D_FF = 4 * D_MODEL
NORM_EPS = 1e-6

BATCH_TILE = SUBLANES
SEQ_TILE = 64
MIX_ROWS = BATCH_TILE * SEQ_TILE
SSM_CHUNK = 4
N_CHUNKS = SEQ_TILE // SSM_CHUNK
CHUNK_ROWS = N_CHUNKS * SUBLANES
MLP_ROWS = 1024
FF_CHUNK = 1024

GROUPS_PER_BLOCK = LANES // SSM_GROUP
N_BLOCKS = SSM_GROUPS // GROUPS_PER_BLOCK
BLOCK_STATE = GROUPS_PER_BLOCK * SSM_STATE
STATE_LANES = SSM_GROUPS * SSM_STATE
SCAN_LANES = 512

_O_U = 0
_O_CB = _O_U + SSM_WIDTH
_O_CC = _O_CB + D_MODEL
_O_CV = _O_CC + D_MODEL
_O_GS = _O_CV + D_MODEL
_O_GC = _O_GS + D_MODEL
IN_COLS = _O_GC + D_MODEL


def _rmsnorm(x, g):
    var = jnp.mean(x * x, axis=-1, keepdims=True)
    return x * lax.rsqrt(var + NORM_EPS) * g


def _dot(a, b):
    return jnp.dot(a, b, preferred_element_type=jnp.float32)


def _mixer_kernel(x_ref, gmix_ref, win_ref, bin_ref, are_ref, aim_ref, bw_ref, cw_ref, tw_ref, dskip_ref,
                  wglu_ref, convw_ref, convb_ref, wco_ref, wout_ref,
                  o_ref,
                  xn_slab, xn16, sr_ref, si_ref, csr_ref, csi_ref, p_ref, m_slab):
    j = pl.program_id(1)
    n_slab = D_MODEL // LANES
    halo = (CONV_K - 1) * SUBLANES

    @pl.when(j == 0)
    def _():
        csr_ref[...] = jnp.zeros_like(csr_ref)
        csi_ref[...] = jnp.zeros_like(csi_ref)
        p_ref[0:halo, :] = jnp.zeros((halo, D_MODEL), jnp.float32)

    g = gmix_ref[...]
    for b in range(BATCH_TILE):
        xnb = _rmsnorm(x_ref[b], g)
        for k in range(n_slab):
            xn_slab[k, pl.ds(b, SEQ_TILE, stride=SUBLANES), :] = xnb[:, k * LANES:(k + 1) * LANES]
    for k in range(n_slab):
        xn16[:, k * LANES:(k + 1) * LANES] = xn_slab[k].astype(jnp.bfloat16)

    f32 = jnp.float32
    xn = xn16[...]

    u = jnp.dot(xn, win_ref[:, _O_U:_O_CB], preferred_element_type=f32) + bin_ref[:, _O_U:_O_CB]
    u4 = u.reshape(N_CHUNKS, SSM_CHUNK, SUBLANES, SSM_WIDTH)
    ub = []
    for q in range(N_BLOCKS):
        ub.append(jnp.concatenate(
            [u4[:, jj, :, q * LANES:(q + 1) * LANES].reshape(CHUNK_ROWS, LANES) for jj in range(SSM_CHUNK)],
            axis=1).astype(jnp.bfloat16))
        v = jnp.dot(ub[q], bw_ref[q], preferred_element_type=f32)
        sr_ref[:, q * BLOCK_STATE:(q + 1) * BLOCK_STATE] = v[:, :BLOCK_STATE]
        si_ref[:, q * BLOCK_STATE:(q + 1) * BLOCK_STATE] = v[:, BLOCK_STATE:]

    for c in range(STATE_LANES // SCAN_LANES):
        sl = slice(c * SCAN_LANES, (c + 1) * SCAN_LANES)
        ar = jnp.broadcast_to(are_ref[:, sl], (SUBLANES, SCAN_LANES))
        ai = jnp.broadcast_to(aim_ref[:, sl], (SUBLANES, SCAN_LANES))
        cr, ci = csr_ref[:, sl], csi_ref[:, sl]
        for k in range(N_CHUNKS):
            rows = slice(k * SUBLANES, (k + 1) * SUBLANES)
            vr, vi = sr_ref[rows, sl], si_ref[rows, sl]
            sr_ref[rows, sl] = cr
            si_ref[rows, sl] = ci
            cr, ci = ar * cr - ai * ci + vr, ar * ci + ai * cr + vi
        csr_ref[:, sl] = cr
        csi_ref[:, sl] = ci

    ys = []
    for q in range(N_BLOCKS):
        sq = jnp.concatenate([sr_ref[:, q * BLOCK_STATE:(q + 1) * BLOCK_STATE],
                              si_ref[:, q * BLOCK_STATE:(q + 1) * BLOCK_STATE]], axis=1)
        yb = jnp.dot(sq.astype(jnp.bfloat16), cw_ref[q], preferred_element_type=f32)
        yb = yb + jnp.dot(ub[q], tw_ref[q], preferred_element_type=f32)
        ys.append(jnp.concatenate(
            [yb[:, jj * LANES:(jj + 1) * LANES].reshape(N_CHUNKS, 1, SUBLANES, LANES)
             for jj in range(SSM_CHUNK)], axis=1).reshape(MIX_ROWS, LANES))
    y_ssm = jnp.concatenate(ys, axis=1) + dskip_ref[...] * u
    z = jax.nn.gelu(y_ssm, approximate=True).astype(jnp.bfloat16)
    ab = jnp.dot(z, wglu_ref[...], preferred_element_type=f32)
    y_a = jnp.concatenate(
        [ab[:, 2 * k * LANES:(2 * k + 1) * LANES] * jax.nn.sigmoid(ab[:, (2 * k + 1) * LANES:(2 * k + 2) * LANES])
         for k in range(n_slab)], axis=1)
    g_ssm = jnp.dot(xn, win_ref[:, _O_GS:_O_GC], preferred_element_type=f32) + bin_ref[:, _O_GS:_O_GC]
    merged = jax.nn.sigmoid(g_ssm) * y_a

    ccv = jnp.dot(xn, win_ref[:, _O_CC:_O_GS], preferred_element_type=f32) + bin_ref[:, _O_CC:_O_GS]
    p_ref[halo:halo + MIX_ROWS, :] = jnp.concatenate(
        [ccv[:, 2 * k * LANES:(2 * k + 1) * LANES] * ccv[:, (2 * k + 1) * LANES:(2 * k + 2) * LANES]
         for k in range(n_slab)], axis=1)
    conv = convb_ref[...]
    for k in range(CONV_K):
        conv = conv + convw_ref[k:k + 1, :] * p_ref[k * SUBLANES:k * SUBLANES + MIX_ROWS, :]
    p_ref[0:halo, :] = p_ref[MIX_ROWS:MIX_ROWS + halo, :]
    c_b = jnp.dot(xn, win_ref[:, _O_CB:_O_CC], preferred_element_type=f32) + bin_ref[:, _O_CB:_O_CC]
    y_b = jnp.dot((c_b * conv).astype(jnp.bfloat16), wco_ref[...], preferred_element_type=f32)
    g_conv = jnp.dot(xn, win_ref[:, _O_GC:IN_COLS], preferred_element_type=f32) + bin_ref[:, _O_GC:IN_COLS]
    merged = merged + jax.nn.sigmoid(g_conv) * y_b

    mix = jnp.dot(merged.astype(jnp.bfloat16), wout_ref[...], preferred_element_type=f32)
    for k in range(n_slab):
        m_slab[k] = mix[:, k * LANES:(k + 1) * LANES]
    for b in range(BATCH_TILE):
        mb = jnp.concatenate(
            [m_slab[k, pl.ds(b, SEQ_TILE, stride=SUBLANES), :] for k in range(n_slab)], axis=1)
        o_ref[b] = x_ref[b] + mb


def _resident(shape):
    return pl.BlockSpec(shape, lambda *_: (0,) * len(shape), pipeline_mode=pl.Buffered(1))


def _mixer(x, gmix, win, b_in, a_re, a_im, bw, cw, tw, dskip, wglu, convw, convb, wco, wout):
    batch, seq, d = x.shape
    assert d == D_MODEL and batch % BATCH_TILE == 0 and seq % SEQ_TILE == 0
    tile = pl.BlockSpec((BATCH_TILE, SEQ_TILE, D_MODEL), lambda i, j: (i, j, 0))
    weights = (gmix, win, b_in, a_re, a_im, bw, cw, tw, dskip, wglu, convw, convb, wco, wout)
    return pl.pallas_call(
        _mixer_kernel,
        out_shape=jax.ShapeDtypeStruct(x.shape, jnp.float32),
        grid=(batch // BATCH_TILE, seq // SEQ_TILE),
        in_specs=[tile] + [_resident(w.shape) for w in weights],
        out_specs=tile,
        scratch_shapes=[
            pltpu.VMEM((D_MODEL // LANES, MIX_ROWS, LANES), jnp.float32),
            pltpu.VMEM((MIX_ROWS, D_MODEL), jnp.bfloat16),
            pltpu.VMEM((CHUNK_ROWS, STATE_LANES), jnp.float32),
            pltpu.VMEM((CHUNK_ROWS, STATE_LANES), jnp.float32),
            pltpu.VMEM((SUBLANES, STATE_LANES), jnp.float32),
            pltpu.VMEM((SUBLANES, STATE_LANES), jnp.float32),
            pltpu.VMEM((MIX_ROWS + (CONV_K - 1) * SUBLANES, D_MODEL), jnp.float32),
            pltpu.VMEM((D_MODEL // LANES, MIX_ROWS, LANES), jnp.float32),
        ],
        compiler_params=pltpu.CompilerParams(
            dimension_semantics=("arbitrary", "arbitrary"),
            vmem_limit_bytes=VMEM_LIMIT_BYTES),
        name="mixer",
    )(x, *weights)


def _mlp_kernel(h_ref, gmlp_ref, w1_ref, w2_ref, gfin_ref, o_ref):
    h = h_ref[...]
    hn = _rmsnorm(h, gmlp_ref[...]).astype(jnp.bfloat16)
    acc = h
    for c in range(D_FF // FF_CHUNK):
        a = jnp.maximum(_dot(hn, w1_ref[:, c * FF_CHUNK:(c + 1) * FF_CHUNK]), 0.0)
        acc = acc + _dot((a * a).astype(jnp.bfloat16), w2_ref[c * FF_CHUNK:(c + 1) * FF_CHUNK, :])
    o_ref[...] = _rmsnorm(acc, gfin_ref[...])


def _mlp(h, gmlp, w1, w2, gfin):
    rows, d = h.shape
    assert d == D_MODEL and rows % MLP_ROWS == 0
    tile = pl.BlockSpec((MLP_ROWS, D_MODEL), lambda i: (i, 0))
    weights = (gmlp, w1, w2, gfin)
    in_specs = [tile, _resident(gmlp.shape), _resident(w1.shape), _resident(w2.shape), _resident(gfin.shape)]
    return pl.pallas_call(
        _mlp_kernel,
        out_shape=jax.ShapeDtypeStruct(h.shape, jnp.float32),
        grid=(rows // MLP_ROWS,),
        in_specs=in_specs,
        out_specs=tile,
        compiler_params=pltpu.CompilerParams(
            dimension_semantics=("arbitrary",),
            vmem_limit_bytes=VMEM_LIMIT_BYTES),
        name="mlp",
    )(h, *weights)


def _ssm_weights(lam_re, lam_im, log_dt, b_re, b_im, c_re, c_im):
    n = SSM_CHUNK
    hi = lax.Precision.HIGHEST
    eye = jnp.eye(GROUPS_PER_BLOCK, dtype=jnp.float32)
    blk = (N_BLOCKS, GROUPS_PER_BLOCK)

    dt = jnp.exp(log_dt)[:, None]

    def a_pow(m):
        m = jnp.asarray(m, jnp.float32)[..., None, None]
        mag = jnp.exp(m * (lam_re * dt))
        return mag * jnp.cos(m * (lam_im * dt)), mag * jnp.sin(m * (lam_im * dt))

    ab_re, ab_im = a_pow(1)
    er, ei = ab_re - 1.0, ab_im
    den = lam_re * lam_re + lam_im * lam_im
    q_re = (er * lam_re + ei * lam_im) / den
    q_im = (ei * lam_re - er * lam_im) / den
    bb_re = q_re[..., None] * b_re - q_im[..., None] * b_im
    bb_im = q_re[..., None] * b_im + q_im[..., None] * b_re

    pr, pi = a_pow(n - 1 - np.arange(n))
    pr, pi = pr[..., None], pi[..., None]
    ab = jnp.stack([pr * bb_re - pi * bb_im, pr * bb_im + pi * bb_re])
    bw = jnp.einsum('pjqgnc,gh->qjgcphn', ab.reshape(2, n, *blk, SSM_STATE, SSM_GROUP), eye,
                    precision=hi).reshape(N_BLOCKS, n * LANES, 2 * BLOCK_STATE)

    def c_a_pow(m):
        pr, pi = a_pow(m)
        pr, pi = pr[..., None, :], pi[..., None, :]
        return c_re * pr - c_im * pi, c_re * pi + c_im * pr

    car, cai = c_a_pow(1 + np.arange(n))
    co = jnp.stack([car, -cai])
    cw = jnp.einsum('pjqgcn,gh->qpgnjhc', co.reshape(2, n, *blk, SSM_GROUP, SSM_STATE), eye,
                    precision=hi).reshape(N_BLOCKS, 2 * BLOCK_STATE, n * LANES)

    lag = np.arange(n)[None, :] - np.arange(n)[:, None]
    car, cai = c_a_pow(np.maximum(lag, 0))
    k = (jnp.einsum('ijgon,gnc->ijgoc', car, bb_re, precision=hi)
         - jnp.einsum('ijgon,gnc->ijgoc', cai, bb_im, precision=hi))
    k = k * jnp.asarray(lag >= 0, jnp.float32)[:, :, None, None, None]
    tw = jnp.einsum('ijqgoc,gh->qigcjho', k.reshape(n, n, *blk, SSM_GROUP, SSM_GROUP), eye,
                    precision=hi).reshape(N_BLOCKS, n * LANES, n * LANES)
    an_re, an_im = a_pow(n)
    return an_re, an_im, bw, cw, tw


def _interleave(a, b):
    r, n = a.shape
    return jnp.stack([a.reshape(r, n // LANES, LANES), b.reshape(r, n // LANES, LANES)],
                     axis=2).reshape(r, 2 * n)


def _layer(h, norm_mix_g, w_in, b_in, lam_re, lam_im, log_dt, ssm_b_re, ssm_b_im, ssm_c_re, ssm_c_im,
           ssm_d, w_glu_a, w_glu_b, conv_w, conv_b, w_conv_out, w_out):
    bf16 = jnp.bfloat16
    an_re, an_im, bw, cw, tw = _ssm_weights(lam_re, lam_im, log_dt, ssm_b_re, ssm_b_im, ssm_c_re, ssm_c_im)
    b_in = b_in.reshape(1, IN_COLS)
    w_in, b_in = (jnp.concatenate(
        [a[:, :_O_CC], _interleave(a[:, _O_CC:_O_CV], a[:, _O_CV:_O_GS]), a[:, _O_GS:]], axis=1)
        for a in (w_in, b_in))
    return _mixer(
        h, norm_mix_g.reshape(1, D_MODEL), w_in.astype(bf16), b_in,
        an_re.reshape(1, STATE_LANES), an_im.reshape(1, STATE_LANES),
        bw.astype(bf16), cw.astype(bf16), tw.astype(bf16),
        ssm_d.reshape(1, SSM_WIDTH), _interleave(w_glu_a, w_glu_b).astype(bf16),
        conv_w, conv_b.reshape(1, D_MODEL), w_conv_out.astype(bf16), w_out.astype(bf16))


def kernel(x, norm_mix_g, w_in, b_in, lam_re, lam_im, log_dt, ssm_b_re, ssm_b_im, ssm_c_re, ssm_c_im,
           ssm_d, w_glu_a, w_glu_b, conv_w, conv_b, w_conv_out, w_out, norm_mlp_g, w_ff1, w_ff2,
           norm_final_g):
    depth = norm_mix_g.shape[0]
    assert depth == 1, "the MLP call applies the final norm, so it must be the last layer"
    batch, seq, d = x.shape
    l = 0
    h = _layer(x, norm_mix_g[l], w_in[l], b_in[l], lam_re[l], lam_im[l], log_dt[l],
               ssm_b_re[l], ssm_b_im[l], ssm_c_re[l], ssm_c_im[l], ssm_d[l],
               w_glu_a[l], w_glu_b[l], conv_w[l], conv_b[l], w_conv_out[l], w_out[l])
    out = _mlp(h.reshape(batch * seq, d), norm_mlp_g[l].reshape(1, d), w_ff1[l].astype(jnp.bfloat16),
               w_ff2[l].astype(jnp.bfloat16), norm_final_g.reshape(1, d))
    return out.reshape(batch, seq, d)
```

```python
import numpy as np
import jax
import jax.numpy as jnp
from jax import lax
from jax.experimental import pallas as pl
from jax.experimental.pallas import tpu as pltpu

LANES = 128
SUBLANES = 8
VMEM_LIMIT_BYTES = 56 * 1024 * 1024

D_MODEL = 1024
SSM_GROUP = 16
SSM_WIDTH = D_MODEL // 2
SSM_GROUPS = SSM_WIDTH // SSM_GROUP
SSM_STATE = 64
CONV_K = 3
D_FF = 4 * D_MODEL
NORM_EPS = 1e-6

BATCH_TILE = SUBLANES
SEQ_TILE = 64
MIX_ROWS = BATCH_TILE * SEQ_TILE
SSM_CHUNK = 4
N_CHUNKS = SEQ_TILE // SSM_CHUNK
CHUNK_ROWS = N_CHUNKS * SUBLANES
MLP_ROWS = 1024
FF_CHUNK = 1024

GROUPS_PER_BLOCK = LANES // SSM_GROUP
N_BLOCKS = SSM_GROUPS // GROUPS_PER_BLOCK
BLOCK_STATE = GROUPS_PER_BLOCK * SSM_STATE
STATE_LANES = SSM_GROUPS * SSM_STATE
SCAN_LANES = 512

_O_U = 0
_O_CB = _O_U + SSM_WIDTH
_O_CC = _O_CB + D_MODEL
_O_CV = _O_CC + D_MODEL
_O_GS = _O_CV + D_MODEL
_O_GC = _O_GS + D_MODEL
IN_COLS = _O_GC + D_MODEL


def _rmsnorm(x, g):
    var = jnp.mean(x * x, axis=-1, keepdims=True)
    return x * lax.rsqrt(var + NORM_EPS) * g


def _dot(a, b):
    return jnp.dot(a, b, preferred_element_type=jnp.float32)


def _mixer_kernel(x_ref, gmix_ref, win_ref, bin_ref, are_ref, aim_ref, bw_ref, cw_ref, tw_ref, dskip_ref,
                  wglu_ref, convw_ref, convb_ref, wco_ref, wout_ref,
                  o_ref,
                  xn_slab, xn16, sr_ref, si_ref, csr_ref, csi_ref, p_ref, m_slab):
    j = pl.program_id(1)
    n_slab = D_MODEL // LANES
    halo = (CONV_K - 1) * SUBLANES

    @pl.when(j == 0)
    def _():
        csr_ref[...] = jnp.zeros_like(csr_ref)
        csi_ref[...] = jnp.zeros_like(csi_ref)
        p_ref[0:halo, :] = jnp.zeros((halo, D_MODEL), jnp.float32)

    g = gmix_ref[...]
    for b in range(BATCH_TILE):
        xnb = _rmsnorm(x_ref[b], g)
        for k in range(n_slab):
            xn_slab[k, pl.ds(b, SEQ_TILE, stride=SUBLANES), :] = xnb[:, k * LANES:(k + 1) * LANES]
    for k in range(n_slab):
        xn16[:, k * LANES:(k + 1) * LANES] = xn_slab[k].astype(jnp.bfloat16)

    f32 = jnp.float32
    xn = xn16[...]

    u = jnp.dot(xn, win_ref[:, _O_U:_O_CB], preferred_element_type=f32) + bin_ref[:, _O_U:_O_CB]
    u4 = u.reshape(N_CHUNKS, SSM_CHUNK, SUBLANES, SSM_WIDTH)
    ub = []
    for q in range(N_BLOCKS):
        ub.append(jnp.concatenate(
            [u4[:, jj, :, q * LANES:(q + 1) * LANES].reshape(CHUNK_ROWS, LANES) for jj in range(SSM_CHUNK)],
            axis=1).astype(jnp.bfloat16))
        v = jnp.dot(ub[q], bw_ref[q], preferred_element_type=f32)
        sr_ref[:, q * BLOCK_STATE:(q + 1) * BLOCK_STATE] = v[:, :BLOCK_STATE]
        si_ref[:, q * BLOCK_STATE:(q + 1) * BLOCK_STATE] = v[:, BLOCK_STATE:]

    for c in range(STATE_LANES // SCAN_LANES):
        sl = slice(c * SCAN_LANES, (c + 1) * SCAN_LANES)
        ar = jnp.broadcast_to(are_ref[:, sl], (SUBLANES, SCAN_LANES))
        ai = jnp.broadcast_to(aim_ref[:, sl], (SUBLANES, SCAN_LANES))
        cr, ci = csr_ref[:, sl], csi_ref[:, sl]
        for k in range(N_CHUNKS):
            rows = slice(k * SUBLANES, (k + 1) * SUBLANES)
            vr, vi = sr_ref[rows, sl], si_ref[rows, sl]
            sr_ref[rows, sl] = cr
            si_ref[rows, sl] = ci
            cr, ci = ar * cr - ai * ci + vr, ar * ci + ai * cr + vi
        csr_ref[:, sl] = cr
        csi_ref[:, sl] = ci

    ys = []
    for q in range(N_BLOCKS):
        sq = jnp.concatenate([sr_ref[:, q * BLOCK_STATE:(q + 1) * BLOCK_STATE],
                              si_ref[:, q * BLOCK_STATE:(q + 1) * BLOCK_STATE]], axis=1)
        yb = jnp.dot(sq.astype(jnp.bfloat16), cw_ref[q], preferred_element_type=f32)
        yb = yb + jnp.dot(ub[q], tw_ref[q], preferred_element_type=f32)
        ys.append(jnp.concatenate(
            [yb[:, jj * LANES:(jj + 1) * LANES].reshape(N_CHUNKS, 1, SUBLANES, LANES)
             for jj in range(SSM_CHUNK)], axis=1).reshape(MIX_ROWS, LANES))
    y_ssm = jnp.concatenate(ys, axis=1) + dskip_ref[...] * u
    z = jax.nn.gelu(y_ssm, approximate=True).astype(jnp.bfloat16)
    ab = jnp.dot(z, wglu_ref[...], preferred_element_type=f32)
    y_a = ab[:, :D_MODEL] * jax.nn.sigmoid(ab[:, D_MODEL:])
    g_ssm = jnp.dot(xn, win_ref[:, _O_GS:_O_GC], preferred_element_type=f32) + bin_ref[:, _O_GS:_O_GC]
    merged = jax.nn.sigmoid(g_ssm) * y_a

    c_c = jnp.dot(xn, win_ref[:, _O_CC:_O_CV], preferred_element_type=f32) + bin_ref[:, _O_CC:_O_CV]
    c_v = jnp.dot(xn, win_ref[:, _O_CV:_O_GS], preferred_element_type=f32) + bin_ref[:, _O_CV:_O_GS]
    p_ref[halo:halo + MIX_ROWS, :] = c_c * c_v
    conv = convb_ref[...]
    for k in range(CONV_K):
        conv = conv + convw_ref[k:k + 1, :] * p_ref[k * SUBLANES:k * SUBLANES + MIX_ROWS, :]
    p_ref[0:halo, :] = p_ref[MIX_ROWS:MIX_ROWS + halo, :]
    c_b = jnp.dot(xn, win_ref[:, _O_CB:_O_CC], preferred_element_type=f32) + bin_ref[:, _O_CB:_O_CC]
    y_b = jnp.dot((c_b * conv).astype(jnp.bfloat16), wco_ref[...], preferred_element_type=f32)
    g_conv = jnp.dot(xn, win_ref[:, _O_GC:IN_COLS], preferred_element_type=f32) + bin_ref[:, _O_GC:IN_COLS]
    merged = merged + jax.nn.sigmoid(g_conv) * y_b

    mix = jnp.dot(merged.astype(jnp.bfloat16), wout_ref[...], preferred_element_type=f32)
    for k in range(n_slab):
        m_slab[k] = mix[:, k * LANES:(k + 1) * LANES]
    for b in range(BATCH_TILE):
        mb = jnp.concatenate(
            [m_slab[k, pl.ds(b, SEQ_TILE, stride=SUBLANES), :] for k in range(n_slab)], axis=1)
        o_ref[b] = x_ref[b] + mb


def _resident(shape):
    return pl.BlockSpec(shape, lambda *_: (0,) * len(shape), pipeline_mode=pl.Buffered(1))


def _mixer(x, gmix, win, b_in, a_re, a_im, bw, cw, tw, dskip, wglu, convw, convb, wco, wout):
    batch, seq, d = x.shape
    assert d == D_MODEL and batch % BATCH_TILE == 0 and seq % SEQ_TILE == 0
    tile = pl.BlockSpec((BATCH_TILE, SEQ_TILE, D_MODEL), lambda i, j: (i, j, 0))
    weights = (gmix, win, b_in, a_re, a_im, bw, cw, tw, dskip, wglu, convw, convb, wco, wout)
    return pl.pallas_call(
        _mixer_kernel,
        out_shape=jax.ShapeDtypeStruct(x.shape, jnp.float32),
        grid=(batch // BATCH_TILE, seq // SEQ_TILE),
        in_specs=[tile] + [_resident(w.shape) for w in weights],
        out_specs=tile,
        scratch_shapes=[
            pltpu.VMEM((D_MODEL // LANES, MIX_ROWS, LANES), jnp.float32),
            pltpu.VMEM((MIX_ROWS, D_MODEL), jnp.bfloat16),
            pltpu.VMEM((CHUNK_ROWS, STATE_LANES), jnp.float32),
            pltpu.VMEM((CHUNK_ROWS, STATE_LANES), jnp.float32),
            pltpu.VMEM((SUBLANES, STATE_LANES), jnp.float32),
            pltpu.VMEM((SUBLANES, STATE_LANES), jnp.float32),
            pltpu.VMEM((MIX_ROWS + (CONV_K - 1) * SUBLANES, D_MODEL), jnp.float32),
            pltpu.VMEM((D_MODEL // LANES, MIX_ROWS, LANES), jnp.float32),
        ],
        compiler_params=pltpu.CompilerParams(
            dimension_semantics=("arbitrary", "arbitrary"),
            vmem_limit_bytes=VMEM_LIMIT_BYTES),
        name="mixer",
    )(x, *weights)


def _mlp_kernel(h_ref, gmlp_ref, w1_ref, w2_ref, gfin_ref, o_ref):
    h = h_ref[...]
    hn = _rmsnorm(h, gmlp_ref[...]).astype(jnp.bfloat16)
    acc = h
    for c in range(D_FF // FF_CHUNK):
        a = jnp.maximum(_dot(hn, w1_ref[:, c * FF_CHUNK:(c + 1) * FF_CHUNK]), 0.0)
        acc = acc + _dot((a * a).astype(jnp.bfloat16), w2_ref[c * FF_CHUNK:(c + 1) * FF_CHUNK, :])
    o_ref[...] = _rmsnorm(acc, gfin_ref[...])


def _mlp(h, gmlp, w1, w2, gfin):
    rows, d = h.shape
    assert d == D_MODEL and rows % MLP_ROWS == 0
    tile = pl.BlockSpec((MLP_ROWS, D_MODEL), lambda i: (i, 0))
    weights = (gmlp, w1, w2, gfin)
    in_specs = [tile, _resident(gmlp.shape), _resident(w1.shape), _resident(w2.shape), _resident(gfin.shape)]
    return pl.pallas_call(
        _mlp_kernel,
        out_shape=jax.ShapeDtypeStruct(h.shape, jnp.float32),
        grid=(rows // MLP_ROWS,),
        in_specs=in_specs,
        out_specs=tile,
        compiler_params=pltpu.CompilerParams(
            dimension_semantics=("arbitrary",),
            vmem_limit_bytes=VMEM_LIMIT_BYTES),
        name="mlp",
    )(h, *weights)


def _ssm_weights(lam_re, lam_im, log_dt, b_re, b_im, c_re, c_im):
    n = SSM_CHUNK
    dt = jnp.exp(log_dt)[:, None]

    def a_pow(m):
        m = jnp.asarray(m, jnp.float32)[..., None, None]
        mag = jnp.exp(m * (lam_re * dt))
        return mag * jnp.cos(m * (lam_im * dt)), mag * jnp.sin(m * (lam_im * dt))

    ab_re, ab_im = a_pow(1)
    er, ei = ab_re - 1.0, ab_im
    den = lam_re * lam_re + lam_im * lam_im
    q_re = (er * lam_re + ei * lam_im) / den
    q_im = (ei * lam_re - er * lam_im) / den
    bb_re = q_re[..., None] * b_re - q_im[..., None] * b_im
    bb_im = q_re[..., None] * b_im + q_im[..., None] * b_re

    pr, pi = a_pow(n - 1 - np.arange(n))
    pr, pi = pr[..., None], pi[..., None]
    bw = _group_block_diag(jnp.stack([pr * bb_re - pi * bb_im, pr * bb_im + pi * bb_re]))

    pr, pi = a_pow(np.arange(n + 1))
    pr, pi = pr[..., None], pi[..., None]
    c_re, c_im = c_re.transpose(0, 2, 1), c_im.transpose(0, 2, 1)
    cw_all = _group_block_diag(jnp.stack([c_re * pr - c_im * pi, -(c_re * pi + c_im * pr)]))
    cw_all = cw_all.transpose(0, 2, 1)

    k = jnp.einsum('qrs,qsc->qrc', bw[:, (n - 1) * LANES:, :], cw_all[:, :, :n * LANES],
                   precision=lax.Precision.HIGHEST)
    tw = jnp.concatenate(
        [jnp.concatenate([jnp.zeros((N_BLOCKS, LANES, i * LANES), k.dtype), k[:, :, :(n - i) * LANES]], axis=2)
         for i in range(n)], axis=1)
    an_re, an_im = a_pow(n)
    return an_re, an_im, bw, cw_all[:, :, LANES:], tw


def _group_block_diag(vals):
    parts, steps = vals.shape[:2]
    pair = LANES // SSM_STATE
    x = vals.reshape(parts, steps, N_BLOCKS, GROUPS_PER_BLOCK, SSM_STATE, SSM_GROUP)
    x = x.transpose(2, 1, 3, 5, 0, 4)
    half = np.arange(GROUPS_PER_BLOCK)[:, None] % pair == np.arange(pair)[None, :]
    x = jnp.where(half[None, None, :, None, None, :, None], x[:, :, :, :, :, None, :], 0.0)
    x = x.reshape(N_BLOCKS, steps * LANES, parts, LANES)
    tiled = jnp.concatenate(
        [x[:, :, p, :] for p in range(parts) for _ in range(BLOCK_STATE // LANES)], axis=-1)
    row_pair = (np.arange(steps * LANES) % LANES) // (pair * SSM_GROUP)
    col_tile = (np.arange(parts * BLOCK_STATE) % BLOCK_STATE) // LANES
    return jnp.where(row_pair[:, None] == col_tile[None, :], tiled, 0.0)


def _layer(h, norm_mix_g, w_in, b_in, lam_re, lam_im, log_dt, ssm_b_re, ssm_b_im, ssm_c_re, ssm_c_im,
           ssm_d, w_glu_a, w_glu_b, conv_w, conv_b, w_conv_out, w_out):
    bf16 = jnp.bfloat16
    an_re, an_im, bw, cw, tw = _ssm_weights(lam_re, lam_im, log_dt, ssm_b_re, ssm_b_im, ssm_c_re, ssm_c_im)
    return _mixer(
        h, norm_mix_g.reshape(1, D_MODEL), w_in.astype(bf16), b_in.reshape(1, IN_COLS),
        an_re.reshape(1, STATE_LANES), an_im.reshape(1, STATE_LANES),
        bw.astype(bf16), cw.astype(bf16), tw.astype(bf16),
        ssm_d.reshape(1, SSM_WIDTH), jnp.concatenate([w_glu_a, w_glu_b], axis=1).astype(bf16),
        conv_w, conv_b.reshape(1, D_MODEL), w_conv_out.astype(bf16), w_out.astype(bf16))


def kernel(x, norm_mix_g, w_in, b_in, lam_re, lam_im, log_dt, ssm_b_re, ssm_b_im, ssm_c_re, ssm_c_im,
           ssm_d, w_glu_a, w_glu_b, conv_w, conv_b, w_conv_out, w_out, norm_mlp_g, w_ff1, w_ff2,
           norm_final_g):
    depth = norm_mix_g.shape[0]
    assert depth == 1, "the MLP call applies the final norm, so it must be the last layer"
    batch, seq, d = x.shape
    l = 0
    h = _layer(x, norm_mix_g[l], w_in[l], b_in[l], lam_re[l], lam_im[l], log_dt[l],
               ssm_b_re[l], ssm_b_im[l], ssm_c_re[l], ssm_c_im[l], ssm_d[l],
               w_glu_a[l], w_glu_b[l], conv_w[l], conv_b[l], w_conv_out[l], w_out[l])
    out = _mlp(h.reshape(batch * seq, d), norm_mlp_g[l].reshape(1, d), w_ff1[l].astype(jnp.bfloat16),
               w_ff2[l].astype(jnp.bfloat16), norm_final_g.reshape(1, d))
    return out.reshape(batch, seq, d)
```

```python
import numpy as np
import jax
import jax.numpy as jnp
from jax import lax
from jax.experimental import pallas as pl
from jax.experimental.pallas import tpu as pltpu

LANES = 128
SUBLANES = 8
VMEM_LIMIT_BYTES = 56 * 1024 * 1024

D_MODEL = 1024
SSM_GROUP = 16
SSM_WIDTH = D_MODEL // 2
SSM_GROUPS = SSM_WIDTH // SSM_GROUP
SSM_STATE = 64
CONV_K = 3
D_FF = 4 * D_MODEL
NORM_EPS = 1e-6

BATCH_TILE = SUBLANES
SEQ_TILE = 64
MIX_ROWS = BATCH_TILE * SEQ_TILE
SSM_CHUNK = 2
N_CHUNKS = SEQ_TILE // SSM_CHUNK
CHUNK_ROWS = N_CHUNKS * SUBLANES
MLP_ROWS = 1024
FF_CHUNK = 1024

GROUPS_PER_BLOCK = LANES // SSM_GROUP
N_BLOCKS = SSM_GROUPS // GROUPS_PER_BLOCK
BLOCK_STATE = GROUPS_PER_BLOCK * SSM_STATE
STATE_LANES = SSM_GROUPS * SSM_STATE
SCAN_LANES = 512

_O_U = 0
_O_CB = _O_U + SSM_WIDTH
_O_CC = _O_CB + D_MODEL
_O_CV = _O_CC + D_MODEL
_O_GS = _O_CV + D_MODEL
_O_GC = _O_GS + D_MODEL
IN_COLS = _O_GC + D_MODEL


def _rmsnorm(x, g):
    var = jnp.mean(x * x, axis=-1, keepdims=True)
    return x * lax.rsqrt(var + NORM_EPS) * g


def _dot(a, b):
    return jnp.dot(a, b, preferred_element_type=jnp.float32)


def _mixer_kernel(x_ref, gmix_ref, win_ref, bin_ref, are_ref, aim_ref, bw_ref, cw_ref, tw_ref, dskip_ref,
                  wglu_ref, convw_ref, convb_ref, wco_ref, wout_ref,
                  o_ref,
                  xn_slab, xn16, sr_ref, si_ref, csr_ref, csi_ref, p_ref, m_slab):
    j = pl.program_id(1)
    n_slab = D_MODEL // LANES
    halo = (CONV_K - 1) * SUBLANES

    @pl.when(j == 0)
    def _():
        csr_ref[...] = jnp.zeros_like(csr_ref)
        csi_ref[...] = jnp.zeros_like(csi_ref)
        p_ref[0:halo, :] = jnp.zeros((halo, D_MODEL), jnp.float32)

    g = gmix_ref[...]
    for b in range(BATCH_TILE):
        xnb = _rmsnorm(x_ref[b], g)
        for k in range(n_slab):
            xn_slab[k, pl.ds(b, SEQ_TILE, stride=SUBLANES), :] = xnb[:, k * LANES:(k + 1) * LANES]
    for k in range(n_slab):
        xn16[:, k * LANES:(k + 1) * LANES] = xn_slab[k].astype(jnp.bfloat16)

    f32 = jnp.float32
    xn = xn16[...]

    u = jnp.dot(xn, win_ref[:, _O_U:_O_CB], preferred_element_type=f32) + bin_ref[:, _O_U:_O_CB]
    u4 = u.reshape(N_CHUNKS, SSM_CHUNK, SUBLANES, SSM_WIDTH)
    ub = []
    for q in range(N_BLOCKS):
        ub.append(jnp.concatenate(
            [u4[:, jj, :, q * LANES:(q + 1) * LANES].reshape(CHUNK_ROWS, LANES) for jj in range(SSM_CHUNK)],
            axis=1).astype(jnp.bfloat16))
        v = jnp.dot(ub[q], bw_ref[q], preferred_element_type=f32)
        sr_ref[:, q * BLOCK_STATE:(q + 1) * BLOCK_STATE] = v[:, :BLOCK_STATE]
        si_ref[:, q * BLOCK_STATE:(q + 1) * BLOCK_STATE] = v[:, BLOCK_STATE:]

    for c in range(STATE_LANES // SCAN_LANES):
        sl = slice(c * SCAN_LANES, (c + 1) * SCAN_LANES)
        ar = jnp.broadcast_to(are_ref[:, sl], (SUBLANES, SCAN_LANES))
        ai = jnp.broadcast_to(aim_ref[:, sl], (SUBLANES, SCAN_LANES))
        cr, ci = csr_ref[:, sl], csi_ref[:, sl]
        for k in range(N_CHUNKS):
            rows = slice(k * SUBLANES, (k + 1) * SUBLANES)
            vr, vi = sr_ref[rows, sl], si_ref[rows, sl]
            sr_ref[rows, sl] = cr
            si_ref[rows, sl] = ci
            cr, ci = ar * cr - ai * ci + vr, ar * ci + ai * cr + vi
        csr_ref[:, sl] = cr
        csi_ref[:, sl] = ci

    ys = []
    for q in range(N_BLOCKS):
        sq = jnp.concatenate([sr_ref[:, q * BLOCK_STATE:(q + 1) * BLOCK_STATE],
                              si_ref[:, q * BLOCK_STATE:(q + 1) * BLOCK_STATE]], axis=1)
        yb = jnp.dot(sq.astype(jnp.bfloat16), cw_ref[q], preferred_element_type=f32)
        yb = yb + jnp.dot(ub[q], tw_ref[q], preferred_element_type=f32)
        ys.append(jnp.concatenate(
            [yb[:, jj * LANES:(jj + 1) * LANES].reshape(N_CHUNKS, 1, SUBLANES, LANES)
             for jj in range(SSM_CHUNK)], axis=1).reshape(MIX_ROWS, LANES))
    y_ssm = jnp.concatenate(ys, axis=1) + dskip_ref[...] * u
    z = jax.nn.gelu(y_ssm, approximate=True).astype(jnp.bfloat16)
    ab = jnp.dot(z, wglu_ref[...], preferred_element_type=f32)
    y_a = ab[:, :D_MODEL] * jax.nn.sigmoid(ab[:, D_MODEL:])
    g_ssm = jnp.dot(xn, win_ref[:, _O_GS:_O_GC], preferred_element_type=f32) + bin_ref[:, _O_GS:_O_GC]
    merged = jax.nn.sigmoid(g_ssm) * y_a

    c_c = jnp.dot(xn, win_ref[:, _O_CC:_O_CV], preferred_element_type=f32) + bin_ref[:, _O_CC:_O_CV]
    c_v = jnp.dot(xn, win_ref[:, _O_CV:_O_GS], preferred_element_type=f32) + bin_ref[:, _O_CV:_O_GS]
    p_ref[halo:halo + MIX_ROWS, :] = c_c * c_v
    conv = convb_ref[...]
    for k in range(CONV_K):
        conv = conv + convw_ref[k:k + 1, :] * p_ref[k * SUBLANES:k * SUBLANES + MIX_ROWS, :]
    p_ref[0:halo, :] = p_ref[MIX_ROWS:MIX_ROWS + halo, :]
    c_b = jnp.dot(xn, win_ref[:, _O_CB:_O_CC], preferred_element_type=f32) + bin_ref[:, _O_CB:_O_CC]
    y_b = jnp.dot((c_b * conv).astype(jnp.bfloat16), wco_ref[...], preferred_element_type=f32)
    g_conv = jnp.dot(xn, win_ref[:, _O_GC:IN_COLS], preferred_element_type=f32) + bin_ref[:, _O_GC:IN_COLS]
    merged = merged + jax.nn.sigmoid(g_conv) * y_b

    mix = jnp.dot(merged.astype(jnp.bfloat16), wout_ref[...], preferred_element_type=f32)
    for k in range(n_slab):
        m_slab[k] = mix[:, k * LANES:(k + 1) * LANES]
    for b in range(BATCH_TILE):
        mb = jnp.concatenate(
            [m_slab[k, pl.ds(b, SEQ_TILE, stride=SUBLANES), :] for k in range(n_slab)], axis=1)
        o_ref[b] = x_ref[b] + mb


def _resident(shape):
    return pl.BlockSpec(shape, lambda *_: (0,) * len(shape), pipeline_mode=pl.Buffered(1))


def _mixer(x, gmix, win, b_in, a_re, a_im, bw, cw, tw, dskip, wglu, convw, convb, wco, wout):
    batch, seq, d = x.shape
    assert d == D_MODEL and batch % BATCH_TILE == 0 and seq % SEQ_TILE == 0
    tile = pl.BlockSpec((BATCH_TILE, SEQ_TILE, D_MODEL), lambda i, j: (i, j, 0))
    weights = (gmix, win, b_in, a_re, a_im, bw, cw, tw, dskip, wglu, convw, convb, wco, wout)
    return pl.pallas_call(
        _mixer_kernel,
        out_shape=jax.ShapeDtypeStruct(x.shape, jnp.float32),
        grid=(batch // BATCH_TILE, seq // SEQ_TILE),
        in_specs=[tile] + [_resident(w.shape) for w in weights],
        out_specs=tile,
        scratch_shapes=[
            pltpu.VMEM((D_MODEL // LANES, MIX_ROWS, LANES), jnp.float32),
            pltpu.VMEM((MIX_ROWS, D_MODEL), jnp.bfloat16),
            pltpu.VMEM((CHUNK_ROWS, STATE_LANES), jnp.float32),
            pltpu.VMEM((CHUNK_ROWS, STATE_LANES), jnp.float32),
            pltpu.VMEM((SUBLANES, STATE_LANES), jnp.float32),
            pltpu.VMEM((SUBLANES, STATE_LANES), jnp.float32),
            pltpu.VMEM((MIX_ROWS + (CONV_K - 1) * SUBLANES, D_MODEL), jnp.float32),
            pltpu.VMEM((D_MODEL // LANES, MIX_ROWS, LANES), jnp.float32),
        ],
        compiler_params=pltpu.CompilerParams(
            dimension_semantics=("arbitrary", "arbitrary"),
            vmem_limit_bytes=VMEM_LIMIT_BYTES),
        name="mixer",
    )(x, *weights)


def _mlp_kernel(h_ref, gmlp_ref, w1_ref, w2_ref, gfin_ref, o_ref):
    h = h_ref[...]
    hn = _rmsnorm(h, gmlp_ref[...]).astype(jnp.bfloat16)
    acc = h
    for c in range(D_FF // FF_CHUNK):
        a = jnp.maximum(_dot(hn, w1_ref[:, c * FF_CHUNK:(c + 1) * FF_CHUNK]), 0.0)
        acc = acc + _dot((a * a).astype(jnp.bfloat16), w2_ref[c * FF_CHUNK:(c + 1) * FF_CHUNK, :])
    o_ref[...] = _rmsnorm(acc, gfin_ref[...])


def _mlp(h, gmlp, w1, w2, gfin):
    rows, d = h.shape
    assert d == D_MODEL and rows % MLP_ROWS == 0
    tile = pl.BlockSpec((MLP_ROWS, D_MODEL), lambda i: (i, 0))
    weights = (gmlp, w1, w2, gfin)
    in_specs = [tile, _resident(gmlp.shape), _resident(w1.shape), _resident(w2.shape), _resident(gfin.shape)]
    return pl.pallas_call(
        _mlp_kernel,
        out_shape=jax.ShapeDtypeStruct(h.shape, jnp.float32),
        grid=(rows // MLP_ROWS,),
        in_specs=in_specs,
        out_specs=tile,
        compiler_params=pltpu.CompilerParams(
            dimension_semantics=("arbitrary",),
            vmem_limit_bytes=VMEM_LIMIT_BYTES),
        name="mlp",
    )(h, *weights)


def _ssm_weights(lam_re, lam_im, log_dt, b_re, b_im, c_re, c_im):
    n = SSM_CHUNK
    dt = jnp.exp(log_dt)[:, None]

    def a_pow(m):
        m = jnp.asarray(m, jnp.float32)[..., None, None]
        mag = jnp.exp(m * (lam_re * dt))
        return mag * jnp.cos(m * (lam_im * dt)), mag * jnp.sin(m * (lam_im * dt))

    ab_re, ab_im = a_pow(1)
    er, ei = ab_re - 1.0, ab_im
    den = lam_re * lam_re + lam_im * lam_im
    q_re = (er * lam_re + ei * lam_im) / den
    q_im = (ei * lam_re - er * lam_im) / den
    bb_re = q_re[..., None] * b_re - q_im[..., None] * b_im
    bb_im = q_re[..., None] * b_im + q_im[..., None] * b_re

    pr, pi = a_pow(n - 1 - np.arange(n))
    pr, pi = pr[..., None], pi[..., None]
    bw = _group_block_diag(jnp.stack([pr * bb_re - pi * bb_im, pr * bb_im + pi * bb_re]))

    pr, pi = a_pow(np.arange(n + 1))
    pr, pi = pr[..., None], pi[..., None]
    c_re, c_im = c_re.transpose(0, 2, 1), c_im.transpose(0, 2, 1)
    cw_all = _group_block_diag(jnp.stack([c_re * pr - c_im * pi, -(c_re * pi + c_im * pr)]))
    cw_all = cw_all.transpose(0, 2, 1)

    k = jnp.einsum('qrs,qsc->qrc', bw[:, (n - 1) * LANES:, :], cw_all[:, :, :n * LANES],
                   precision=lax.Precision.HIGHEST)
    tw = jnp.concatenate(
        [jnp.concatenate([jnp.zeros((N_BLOCKS, LANES, i * LANES), k.dtype), k[:, :, :(n - i) * LANES]], axis=2)
         for i in range(n)], axis=1)
    an_re, an_im = a_pow(n)
    return an_re, an_im, bw, cw_all[:, :, LANES:], tw


def _group_block_diag(vals):
    parts, steps = vals.shape[:2]
    pair = LANES // SSM_STATE
    x = vals.reshape(parts, steps, N_BLOCKS, GROUPS_PER_BLOCK, SSM_STATE, SSM_GROUP)
    x = x.transpose(2, 1, 3, 5, 0, 4)
    half = np.arange(GROUPS_PER_BLOCK)[:, None] % pair == np.arange(pair)[None, :]
    x = jnp.where(half[None, None, :, None, None, :, None], x[:, :, :, :, :, None, :], 0.0)
    x = x.reshape(N_BLOCKS, steps * LANES, parts, LANES)
    tiled = jnp.concatenate(
        [x[:, :, p, :] for p in range(parts) for _ in range(BLOCK_STATE // LANES)], axis=-1)
    row_pair = (np.arange(steps * LANES) % LANES) // (pair * SSM_GROUP)
    col_tile = (np.arange(parts * BLOCK_STATE) % BLOCK_STATE) // LANES
    return jnp.where(row_pair[:, None] == col_tile[None, :], tiled, 0.0)


def _layer(h, norm_mix_g, w_in, b_in, lam_re, lam_im, log_dt, ssm_b_re, ssm_b_im, ssm_c_re, ssm_c_im,
           ssm_d, w_glu_a, w_glu_b, conv_w, conv_b, w_conv_out, w_out):
    bf16 = jnp.bfloat16
    an_re, an_im, bw, cw, tw = _ssm_weights(lam_re, lam_im, log_dt, ssm_b_re, ssm_b_im, ssm_c_re, ssm_c_im)
    return _mixer(
        h, norm_mix_g.reshape(1, D_MODEL), w_in.astype(bf16), b_in.reshape(1, IN_COLS),
        an_re.reshape(1, STATE_LANES), an_im.reshape(1, STATE_LANES),
        bw.astype(bf16), cw.astype(bf16), tw.astype(bf16),
        ssm_d.reshape(1, SSM_WIDTH), jnp.concatenate([w_glu_a, w_glu_b], axis=1).astype(bf16),
        conv_w, conv_b.reshape(1, D_MODEL), w_conv_out.astype(bf16), w_out.astype(bf16))


def kernel(x, norm_mix_g, w_in, b_in, lam_re, lam_im, log_dt, ssm_b_re, ssm_b_im, ssm_c_re, ssm_c_im,
           ssm_d, w_glu_a, w_glu_b, conv_w, conv_b, w_conv_out, w_out, norm_mlp_g, w_ff1, w_ff2,
           norm_final_g):
    depth = norm_mix_g.shape[0]
    assert depth == 1, "the MLP call applies the final norm, so it must be the last layer"
    batch, seq, d = x.shape
    l = 0
    h = _layer(x, norm_mix_g[l], w_in[l], b_in[l], lam_re[l], lam_im[l], log_dt[l],
               ssm_b_re[l], ssm_b_im[l], ssm_c_re[l], ssm_c_im[l], ssm_d[l],
               w_glu_a[l], w_glu_b[l], conv_w[l], conv_b[l], w_conv_out[l], w_out[l])
    out = _mlp(h.reshape(batch * seq, d), norm_mlp_g[l].reshape(1, d), w_ff1[l].astype(jnp.bfloat16),
               w_ff2[l].astype(jnp.bfloat16), norm_final_g.reshape(1, d))
    return out.reshape(batch, seq, d)
```

```python
import numpy as np
import jax
import jax.numpy as jnp
from jax import lax
from jax.experimental import pallas as pl
from jax.experimental.pallas import tpu as pltpu

LANES = 128
SUBLANES = 8
VMEM_LIMIT_BYTES = 56 * 1024 * 1024

D_MODEL = 1024
SSM_GROUP = 16
SSM_WIDTH = D_MODEL // 2
SSM_GROUPS = SSM_WIDTH // SSM_GROUP
SSM_STATE = 64
CONV_K = 3
D_FF = 4 * D_MODEL
NORM_EPS = 1e-6

BATCH_TILE = SUBLANES
SEQ_TILE = 64
MIX_ROWS = BATCH_TILE * SEQ_TILE
SSM_CHUNK = 2
N_CHUNKS = SEQ_TILE // SSM_CHUNK
CHUNK_ROWS = N_CHUNKS * SUBLANES
MLP_ROWS = 1024
FF_CHUNK = 1024

GROUPS_PER_BLOCK = LANES // SSM_GROUP
N_BLOCKS = SSM_GROUPS // GROUPS_PER_BLOCK
BLOCK_STATE = GROUPS_PER_BLOCK * SSM_STATE
STATE_LANES = SSM_GROUPS * SSM_STATE
SCAN_LANES = 512

_O_U = 0
_O_CB = _O_U + SSM_WIDTH
_O_CC = _O_CB + D_MODEL
_O_CV = _O_CC + D_MODEL
_O_GS = _O_CV + D_MODEL
_O_GC = _O_GS + D_MODEL
IN_COLS = _O_GC + D_MODEL


def _rmsnorm(x, g):
    var = jnp.mean(x * x, axis=-1, keepdims=True)
    return x * lax.rsqrt(var + NORM_EPS) * g


def _dot(a, b):
    return jnp.dot(a, b, preferred_element_type=jnp.float32)


def _tile_copies(hbm, buf, sem, i, j, slot, to_hbm):
    copies = []
    for b in range(BATCH_TILE):
        h = hbm.at[i * BATCH_TILE + b, pl.ds(j * SEQ_TILE, SEQ_TILE), :]
        v = buf.at[slot, :, b, :]
        copies.append(pltpu.make_async_copy(v, h, sem.at[slot]) if to_hbm
                      else pltpu.make_async_copy(h, v, sem.at[slot]))
    return copies


def _mixer_kernel(x_hbm, gmix_ref, win_ref, bin_ref, are_ref, aim_ref, bw_ref, cw_ref, tw_ref, dskip_ref,
                  wglu_ref, convw_ref, convb_ref, wco_ref, wout_ref,
                  o_hbm,
                  xbuf, obuf, in_sem, out_sem, xn16, sr_ref, si_ref, csr_ref, csi_ref, p_ref):
    i, j = pl.program_id(0), pl.program_id(1)
    n_i, n_j = pl.num_programs(0), pl.num_programs(1)
    step = i * n_j + j
    slot = lax.rem(step, 2)
    is_last = step == n_i * n_j - 1
    halo = (CONV_K - 1) * SUBLANES

    @pl.when(step == 0)
    def _():
        for c in _tile_copies(x_hbm, xbuf, in_sem, i, j, slot, False):
            c.start()

    @pl.when(jnp.logical_not(is_last))
    def _():
        wrap = j + 1 == n_j
        for c in _tile_copies(x_hbm, xbuf, in_sem, jnp.where(wrap, i + 1, i), jnp.where(wrap, 0, j + 1),
                              1 - slot, False):
            c.start()

    for c in _tile_copies(x_hbm, xbuf, in_sem, i, j, slot, False):
        c.wait()

    @pl.when(step >= 2)
    def _():
        for c in _tile_copies(o_hbm, obuf, out_sem, i, j, slot, True):
            c.wait()

    @pl.when(j == 0)
    def _():
        csr_ref[...] = jnp.zeros_like(csr_ref)
        csi_ref[...] = jnp.zeros_like(csi_ref)
        p_ref[0:halo, :] = jnp.zeros((halo, D_MODEL), jnp.float32)

    xn16[...] = _rmsnorm(xbuf[slot].reshape(MIX_ROWS, D_MODEL), gmix_ref[...]).astype(jnp.bfloat16)

    f32 = jnp.float32
    xn = xn16[...]

    u = jnp.dot(xn, win_ref[:, _O_U:_O_CB], preferred_element_type=f32) + bin_ref[:, _O_U:_O_CB]
    u4 = u.reshape(N_CHUNKS, SSM_CHUNK, SUBLANES, SSM_WIDTH)
    ub = []
    for q in range(N_BLOCKS):
        ub.append(jnp.concatenate(
            [u4[:, jj, :, q * LANES:(q + 1) * LANES].reshape(CHUNK_ROWS, LANES) for jj in range(SSM_CHUNK)],
            axis=1).astype(jnp.bfloat16))
        v = jnp.dot(ub[q], bw_ref[q], preferred_element_type=f32)
        sr_ref[:, q * BLOCK_STATE:(q + 1) * BLOCK_STATE] = v[:, :BLOCK_STATE]
        si_ref[:, q * BLOCK_STATE:(q + 1) * BLOCK_STATE] = v[:, BLOCK_STATE:]

    for c in range(STATE_LANES // SCAN_LANES):
        sl = slice(c * SCAN_LANES, (c + 1) * SCAN_LANES)
        ar = jnp.broadcast_to(are_ref[:, sl], (SUBLANES, SCAN_LANES))
        ai = jnp.broadcast_to(aim_ref[:, sl], (SUBLANES, SCAN_LANES))
        cr, ci = csr_ref[:, sl], csi_ref[:, sl]
        for k in range(N_CHUNKS):
            rows = slice(k * SUBLANES, (k + 1) * SUBLANES)
            vr, vi = sr_ref[rows, sl], si_ref[rows, sl]
            sr_ref[rows, sl] = cr
            si_ref[rows, sl] = ci
            cr, ci = ar * cr - ai * ci + vr, ar * ci + ai * cr + vi
        csr_ref[:, sl] = cr
        csi_ref[:, sl] = ci

    ys = []
    for q in range(N_BLOCKS):
        sq = jnp.concatenate([sr_ref[:, q * BLOCK_STATE:(q + 1) * BLOCK_STATE],
                              si_ref[:, q * BLOCK_STATE:(q + 1) * BLOCK_STATE]], axis=1)
        yb = jnp.dot(sq.astype(jnp.bfloat16), cw_ref[q], preferred_element_type=f32)
        yb = yb + jnp.dot(ub[q], tw_ref[q], preferred_element_type=f32)
        ys.append(jnp.concatenate(
            [yb[:, jj * LANES:(jj + 1) * LANES].reshape(N_CHUNKS, 1, SUBLANES, LANES)
             for jj in range(SSM_CHUNK)], axis=1).reshape(MIX_ROWS, LANES))
    y_ssm = jnp.concatenate(ys, axis=1) + dskip_ref[...] * u
    z = jax.nn.gelu(y_ssm, approximate=True).astype(jnp.bfloat16)
    ab = jnp.dot(z, wglu_ref[...], preferred_element_type=f32)
    y_a = ab[:, :D_MODEL] * jax.nn.sigmoid(ab[:, D_MODEL:])
    g_ssm = jnp.dot(xn, win_ref[:, _O_GS:_O_GC], preferred_element_type=f32) + bin_ref[:, _O_GS:_O_GC]
    merged = jax.nn.sigmoid(g_ssm) * y_a

    c_c = jnp.dot(xn, win_ref[:, _O_CC:_O_CV], preferred_element_type=f32) + bin_ref[:, _O_CC:_O_CV]
    c_v = jnp.dot(xn, win_ref[:, _O_CV:_O_GS], preferred_element_type=f32) + bin_ref[:, _O_CV:_O_GS]
    p_ref[halo:halo + MIX_ROWS, :] = c_c * c_v
    conv = convb_ref[...]
    for k in range(CONV_K):
        conv = conv + convw_ref[k:k + 1, :] * p_ref[k * SUBLANES:k * SUBLANES + MIX_ROWS, :]
    p_ref[0:halo, :] = p_ref[MIX_ROWS:MIX_ROWS + halo, :]
    c_b = jnp.dot(xn, win_ref[:, _O_CB:_O_CC], preferred_element_type=f32) + bin_ref[:, _O_CB:_O_CC]
    y_b = jnp.dot((c_b * conv).astype(jnp.bfloat16), wco_ref[...], preferred_element_type=f32)
    g_conv = jnp.dot(xn, win_ref[:, _O_GC:IN_COLS], preferred_element_type=f32) + bin_ref[:, _O_GC:IN_COLS]
    merged = merged + jax.nn.sigmoid(g_conv) * y_b

    mix = jnp.dot(merged.astype(jnp.bfloat16), wout_ref[...], preferred_element_type=f32)
    obuf[slot] = (xbuf[slot].reshape(MIX_ROWS, D_MODEL) + mix).reshape(SEQ_TILE, BATCH_TILE, D_MODEL)

    for c in _tile_copies(o_hbm, obuf, out_sem, i, j, slot, True):
        c.start()

    @pl.when(is_last)
    def _():
        for c in _tile_copies(o_hbm, obuf, out_sem, i, j, slot, True):
            c.wait()

    @pl.when(jnp.logical_and(is_last, step >= 1))
    def _():
        for c in _tile_copies(o_hbm, obuf, out_sem, i, j, 1 - slot, True):
            c.wait()


def _resident(shape):
    return pl.BlockSpec(shape, lambda *_: (0,) * len(shape), pipeline_mode=pl.Buffered(1))


def _mixer(x, gmix, win, b_in, a_re, a_im, bw, cw, tw, dskip, wglu, convw, convb, wco, wout):
    batch, seq, d = x.shape
    assert d == D_MODEL and batch % BATCH_TILE == 0 and seq % SEQ_TILE == 0
    weights = (gmix, win, b_in, a_re, a_im, bw, cw, tw, dskip, wglu, convw, convb, wco, wout)
    ring = (2, SEQ_TILE, BATCH_TILE, D_MODEL)
    return pl.pallas_call(
        _mixer_kernel,
        out_shape=jax.ShapeDtypeStruct(x.shape, jnp.float32),
        grid=(batch // BATCH_TILE, seq // SEQ_TILE),
        in_specs=[pl.BlockSpec(memory_space=pl.ANY)] + [_resident(w.shape) for w in weights],
        out_specs=pl.BlockSpec(memory_space=pl.ANY),
        scratch_shapes=[
            pltpu.VMEM(ring, jnp.float32),
            pltpu.VMEM(ring, jnp.float32),
            pltpu.SemaphoreType.DMA((2,)),
            pltpu.SemaphoreType.DMA((2,)),
            pltpu.VMEM((MIX_ROWS, D_MODEL), jnp.bfloat16),
            pltpu.VMEM((CHUNK_ROWS, STATE_LANES), jnp.float32),
            pltpu.VMEM((CHUNK_ROWS, STATE_LANES), jnp.float32),
            pltpu.VMEM((SUBLANES, STATE_LANES), jnp.float32),
            pltpu.VMEM((SUBLANES, STATE_LANES), jnp.float32),
            pltpu.VMEM((MIX_ROWS + (CONV_K - 1) * SUBLANES, D_MODEL), jnp.float32),
        ],
        compiler_params=pltpu.CompilerParams(
            dimension_semantics=("arbitrary", "arbitrary"),
            vmem_limit_bytes=VMEM_LIMIT_BYTES),
        name="mixer",
    )(x, *weights)


def _mlp_kernel(h_ref, gmlp_ref, w1_ref, w2_ref, gfin_ref, o_ref):
    h = h_ref[...]
    hn = _rmsnorm(h, gmlp_ref[...]).astype(jnp.bfloat16)
    acc = h
    for c in range(D_FF // FF_CHUNK):
        a = jnp.maximum(_dot(hn, w1_ref[:, c * FF_CHUNK:(c + 1) * FF_CHUNK]), 0.0)
        acc = acc + _dot((a * a).astype(jnp.bfloat16), w2_ref[c * FF_CHUNK:(c + 1) * FF_CHUNK, :])
    o_ref[...] = _rmsnorm(acc, gfin_ref[...])


def _mlp(h, gmlp, w1, w2, gfin):
    rows, d = h.shape
    assert d == D_MODEL and rows % MLP_ROWS == 0
    tile = pl.BlockSpec((MLP_ROWS, D_MODEL), lambda i: (i, 0))
    weights = (gmlp, w1, w2, gfin)
    in_specs = [tile, _resident(gmlp.shape), _resident(w1.shape), _resident(w2.shape), _resident(gfin.shape)]
    return pl.pallas_call(
        _mlp_kernel,
        out_shape=jax.ShapeDtypeStruct(h.shape, jnp.float32),
        grid=(rows // MLP_ROWS,),
        in_specs=in_specs,
        out_specs=tile,
        compiler_params=pltpu.CompilerParams(
            dimension_semantics=("arbitrary",),
            vmem_limit_bytes=VMEM_LIMIT_BYTES),
        name="mlp",
    )(h, *weights)


def _ssm_weights(lam_re, lam_im, log_dt, b_re, b_im, c_re, c_im):
    n = SSM_CHUNK
    dt = jnp.exp(log_dt)[:, None]

    def a_pow(m):
        m = jnp.asarray(m, jnp.float32)[..., None, None]
        mag = jnp.exp(m * (lam_re * dt))
        return mag * jnp.cos(m * (lam_im * dt)), mag * jnp.sin(m * (lam_im * dt))

    ab_re, ab_im = a_pow(1)
    er, ei = ab_re - 1.0, ab_im
    den = lam_re * lam_re + lam_im * lam_im
    q_re = (er * lam_re + ei * lam_im) / den
    q_im = (ei * lam_re - er * lam_im) / den
    bb_re = q_re[..., None] * b_re - q_im[..., None] * b_im
    bb_im = q_re[..., None] * b_im + q_im[..., None] * b_re

    pr, pi = a_pow(n - 1 - np.arange(n))
    pr, pi = pr[..., None], pi[..., None]
    bw = _group_block_diag(jnp.stack([pr * bb_re - pi * bb_im, pr * bb_im + pi * bb_re]))

    pr, pi = a_pow(np.arange(n + 1))
    pr, pi = pr[..., None], pi[..., None]
    c_re, c_im = c_re.transpose(0, 2, 1), c_im.transpose(0, 2, 1)
    cw_all = _group_block_diag(jnp.stack([c_re * pr - c_im * pi, -(c_re * pi + c_im * pr)]))
    cw_all = cw_all.transpose(0, 2, 1)

    k = jnp.einsum('qrs,qsc->qrc', bw[:, (n - 1) * LANES:, :], cw_all[:, :, :n * LANES],
                   precision=lax.Precision.HIGHEST)
    tw = jnp.concatenate(
        [jnp.concatenate([jnp.zeros((N_BLOCKS, LANES, i * LANES), k.dtype), k[:, :, :(n - i) * LANES]], axis=2)
         for i in range(n)], axis=1)
    an_re, an_im = a_pow(n)
    return an_re, an_im, bw, cw_all[:, :, LANES:], tw


def _group_block_diag(vals):
    parts, steps = vals.shape[:2]
    pair = LANES // SSM_STATE
    x = vals.reshape(parts, steps, N_BLOCKS, GROUPS_PER_BLOCK, SSM_STATE, SSM_GROUP)
    x = x.transpose(2, 1, 3, 5, 0, 4)
    half = np.arange(GROUPS_PER_BLOCK)[:, None] % pair == np.arange(pair)[None, :]
    x = jnp.where(half[None, None, :, None, None, :, None], x[:, :, :, :, :, None, :], 0.0)
    x = x.reshape(N_BLOCKS, steps * LANES, parts, LANES)
    tiled = jnp.concatenate(
        [x[:, :, p, :] for p in range(parts) for _ in range(BLOCK_STATE // LANES)], axis=-1)
    row_pair = (np.arange(steps * LANES) % LANES) // (pair * SSM_GROUP)
    col_tile = (np.arange(parts * BLOCK_STATE) % BLOCK_STATE) // LANES
    return jnp.where(row_pair[:, None] == col_tile[None, :], tiled, 0.0)


def _layer(h, norm_mix_g, w_in, b_in, lam_re, lam_im, log_dt, ssm_b_re, ssm_b_im, ssm_c_re, ssm_c_im,
           ssm_d, w_glu_a, w_glu_b, conv_w, conv_b, w_conv_out, w_out):
    bf16 = jnp.bfloat16
    an_re, an_im, bw, cw, tw = _ssm_weights(lam_re, lam_im, log_dt, ssm_b_re, ssm_b_im, ssm_c_re, ssm_c_im)
    return _mixer(
        h, norm_mix_g.reshape(1, D_MODEL), w_in.astype(bf16), b_in.reshape(1, IN_COLS),
        an_re.reshape(1, STATE_LANES), an_im.reshape(1, STATE_LANES),
        bw.astype(bf16), cw.astype(bf16), tw.astype(bf16),
        ssm_d.reshape(1, SSM_WIDTH), jnp.concatenate([w_glu_a, w_glu_b], axis=1).astype(bf16),
        conv_w, conv_b.reshape(1, D_MODEL), w_conv_out.astype(bf16), w_out.astype(bf16))


def kernel(x, norm_mix_g, w_in, b_in, lam_re, lam_im, log_dt, ssm_b_re, ssm_b_im, ssm_c_re, ssm_c_im,
           ssm_d, w_glu_a, w_glu_b, conv_w, conv_b, w_conv_out, w_out, norm_mlp_g, w_ff1, w_ff2,
           norm_final_g):
    depth = norm_mix_g.shape[0]
    assert depth == 1, "the MLP call applies the final norm, so it must be the last layer"
    batch, seq, d = x.shape
    l = 0
    h = _layer(x, norm_mix_g[l], w_in[l], b_in[l], lam_re[l], lam_im[l], log_dt[l],
               ssm_b_re[l], ssm_b_im[l], ssm_c_re[l], ssm_c_im[l], ssm_d[l],
               w_glu_a[l], w_glu_b[l], conv_w[l], conv_b[l], w_conv_out[l], w_out[l])
    out = _mlp(h.reshape(batch * seq, d), norm_mlp_g[l].reshape(1, d), w_ff1[l].astype(jnp.bfloat16),
               w_ff2[l].astype(jnp.bfloat16), norm_final_g.reshape(1, d))
    return out.reshape(batch, seq, d)
```

```python
import numpy as np
import jax
import jax.numpy as jnp
from jax import lax
from jax.experimental import pallas as pl
from jax.experimental.pallas import tpu as pltpu

LANES = 128
SUBLANES = 8
VMEM_LIMIT_BYTES = 56 * 1024 * 1024

D_MODEL = 1024
SSM_GROUP = 16
SSM_WIDTH = D_MODEL // 2
SSM_GROUPS = SSM_WIDTH // SSM_GROUP
SSM_STATE = 64
CONV_K = 3
D_FF = 4 * D_MODEL
NORM_EPS = 1e-6

BATCH_TILE = SUBLANES
SEQ_TILE = 64
MIX_ROWS = BATCH_TILE * SEQ_TILE
X_SLOTS = 3
SSM_CHUNK = 2
N_CHUNKS = SEQ_TILE // SSM_CHUNK
CHUNK_ROWS = N_CHUNKS * SUBLANES
MLP_ROWS = 1024
FF_CHUNK = 1024

GROUPS_PER_BLOCK = LANES // SSM_GROUP
N_BLOCKS = SSM_GROUPS // GROUPS_PER_BLOCK
BLOCK_STATE = GROUPS_PER_BLOCK * SSM_STATE
STATE_LANES = SSM_GROUPS * SSM_STATE
SCAN_LANES = 512

_O_U = 0
_O_CB = _O_U + SSM_WIDTH
_O_CC = _O_CB + D_MODEL
_O_CV = _O_CC + D_MODEL
_O_GS = _O_CV + D_MODEL
_O_GC = _O_GS + D_MODEL
IN_COLS = _O_GC + D_MODEL


def _rmsnorm(x, g):
    var = jnp.mean(x * x, axis=-1, keepdims=True)
    return x * lax.rsqrt(var + NORM_EPS) * g


def _dot(a, b):
    return jnp.dot(a, b, preferred_element_type=jnp.float32)


def _tile_copies(hbm, buf, sem, step, n_j, slot, to_hbm):
    i, j = lax.div(step, n_j), lax.rem(step, n_j)
    copies = []
    for b in range(BATCH_TILE):
        h = hbm.at[i * BATCH_TILE + b, pl.ds(j * SEQ_TILE, SEQ_TILE), :]
        v = buf.at[slot, :, b, :]
        copies.append(pltpu.make_async_copy(v, h, sem.at[slot]) if to_hbm
                      else pltpu.make_async_copy(h, v, sem.at[slot]))
    return copies


def _mixer_kernel(x_hbm, gmix_ref, win_ref, bin_ref, are_ref, aim_ref, bw_ref, cw_ref, tw_ref, dskip_ref,
                  wglu_ref, convw_ref, convb_ref, wco_ref, wout_ref,
                  o_hbm,
                  xbuf, obuf, in_sem, out_sem, xn16, sr_ref, si_ref, csr_ref, csi_ref, p_ref):
    j = pl.program_id(1)
    n_j = pl.num_programs(1)
    n_steps = pl.num_programs(0) * n_j
    step = pl.program_id(0) * n_j + j
    xslot, xnext = lax.rem(step, X_SLOTS), lax.rem(step + 1, X_SLOTS)
    oslot = lax.rem(step, 2)
    halo = (CONV_K - 1) * SUBLANES

    def x_copies(s):
        return _tile_copies(x_hbm, xbuf, in_sem, s, n_j, lax.rem(s, X_SLOTS), False)

    def out_copies(slot):
        return _tile_copies(o_hbm, obuf, out_sem, step, n_j, slot, True)

    @pl.when(step == 0)
    def _():
        for c in x_copies(0) + x_copies(1):
            c.start()
        for c in x_copies(0):
            c.wait()
        xn16[0] = _rmsnorm(xbuf[0].reshape(MIX_ROWS, D_MODEL), gmix_ref[...]).astype(jnp.bfloat16)

    @pl.when(step + 2 < n_steps)
    def _():
        for c in x_copies(step + 2):
            c.start()

    @pl.when(step + 1 < n_steps)
    def _():
        for c in x_copies(step + 1):
            c.wait()

    @pl.when(step >= 2)
    def _():
        for c in out_copies(oslot):
            c.wait()

    @pl.when(j == 0)
    def _():
        csr_ref[...] = jnp.zeros_like(csr_ref)
        csi_ref[...] = jnp.zeros_like(csi_ref)
        p_ref[0:halo, :] = jnp.zeros((halo, D_MODEL), jnp.float32)

    f32 = jnp.float32
    xn = xn16[oslot]

    u = jnp.dot(xn, win_ref[:, _O_U:_O_CB], preferred_element_type=f32) + bin_ref[:, _O_U:_O_CB]
    u4 = u.reshape(N_CHUNKS, SSM_CHUNK, SUBLANES, SSM_WIDTH)
    ub = []
    for q in range(N_BLOCKS):
        ub.append(jnp.concatenate(
            [u4[:, jj, :, q * LANES:(q + 1) * LANES].reshape(CHUNK_ROWS, LANES) for jj in range(SSM_CHUNK)],
            axis=1).astype(jnp.bfloat16))
        v = jnp.dot(ub[q], bw_ref[q], preferred_element_type=f32)
        sr_ref[:, q * BLOCK_STATE:(q + 1) * BLOCK_STATE] = v[:, :BLOCK_STATE]
        si_ref[:, q * BLOCK_STATE:(q + 1) * BLOCK_STATE] = v[:, BLOCK_STATE:]

    xn16[1 - oslot] = _rmsnorm(xbuf[xnext].reshape(MIX_ROWS, D_MODEL), gmix_ref[...]).astype(jnp.bfloat16)

    for c in range(STATE_LANES // SCAN_LANES):
        sl = slice(c * SCAN_LANES, (c + 1) * SCAN_LANES)
        ar = jnp.broadcast_to(are_ref[:, sl], (SUBLANES, SCAN_LANES))
        ai = jnp.broadcast_to(aim_ref[:, sl], (SUBLANES, SCAN_LANES))
        cr, ci = csr_ref[:, sl], csi_ref[:, sl]
        for k in range(N_CHUNKS):
            rows = slice(k * SUBLANES, (k + 1) * SUBLANES)
            vr, vi = sr_ref[rows, sl], si_ref[rows, sl]
            sr_ref[rows, sl] = cr
            si_ref[rows, sl] = ci
            cr, ci = ar * cr - ai * ci + vr, ar * ci + ai * cr + vi
        csr_ref[:, sl] = cr
        csi_ref[:, sl] = ci

    ys = []
    for q in range(N_BLOCKS):
        sq = jnp.concatenate([sr_ref[:, q * BLOCK_STATE:(q + 1) * BLOCK_STATE],
                              si_ref[:, q * BLOCK_STATE:(q + 1) * BLOCK_STATE]], axis=1)
        yb = jnp.dot(sq.astype(jnp.bfloat16), cw_ref[q], preferred_element_type=f32)
        yb = yb + jnp.dot(ub[q], tw_ref[q], preferred_element_type=f32)
        ys.append(jnp.concatenate(
            [yb[:, jj * LANES:(jj + 1) * LANES].reshape(N_CHUNKS, 1, SUBLANES, LANES)
             for jj in range(SSM_CHUNK)], axis=1).reshape(MIX_ROWS, LANES))
    y_ssm = jnp.concatenate(ys, axis=1) + dskip_ref[...] * u
    z = jax.nn.gelu(y_ssm, approximate=True).astype(jnp.bfloat16)
    ab = jnp.dot(z, wglu_ref[...], preferred_element_type=f32)
    y_a = ab[:, :D_MODEL] * jax.nn.sigmoid(ab[:, D_MODEL:])
    g_ssm = jnp.dot(xn, win_ref[:, _O_GS:_O_GC], preferred_element_type=f32) + bin_ref[:, _O_GS:_O_GC]
    merged = jax.nn.sigmoid(g_ssm) * y_a

    c_c = jnp.dot(xn, win_ref[:, _O_CC:_O_CV], preferred_element_type=f32) + bin_ref[:, _O_CC:_O_CV]
    c_v = jnp.dot(xn, win_ref[:, _O_CV:_O_GS], preferred_element_type=f32) + bin_ref[:, _O_CV:_O_GS]
    p_ref[halo:halo + MIX_ROWS, :] = c_c * c_v
    conv = convb_ref[...]
    for k in range(CONV_K):
        conv = conv + convw_ref[k:k + 1, :] * p_ref[k * SUBLANES:k * SUBLANES + MIX_ROWS, :]
    p_ref[0:halo, :] = p_ref[MIX_ROWS:MIX_ROWS + halo, :]
    c_b = jnp.dot(xn, win_ref[:, _O_CB:_O_CC], preferred_element_type=f32) + bin_ref[:, _O_CB:_O_CC]
    y_b = jnp.dot((c_b * conv).astype(jnp.bfloat16), wco_ref[...], preferred_element_type=f32)
    g_conv = jnp.dot(xn, win_ref[:, _O_GC:IN_COLS], preferred_element_type=f32) + bin_ref[:, _O_GC:IN_COLS]
    merged = merged + jax.nn.sigmoid(g_conv) * y_b

    mix = jnp.dot(merged.astype(jnp.bfloat16), wout_ref[...], preferred_element_type=f32)
    obuf[oslot] = (xbuf[xslot].reshape(MIX_ROWS, D_MODEL) + mix).reshape(SEQ_TILE, BATCH_TILE, D_MODEL)

    for c in out_copies(oslot):
        c.start()

    @pl.when(step == n_steps - 1)
    def _():
        for c in out_copies(oslot) + out_copies(1 - oslot):
            c.wait()


def _resident(shape):
    return pl.BlockSpec(shape, lambda *_: (0,) * len(shape), pipeline_mode=pl.Buffered(1))


def _mixer(x, gmix, win, b_in, a_re, a_im, bw, cw, tw, dskip, wglu, convw, convb, wco, wout):
    batch, seq, d = x.shape
    assert d == D_MODEL and batch % BATCH_TILE == 0 and seq % SEQ_TILE == 0
    weights = (gmix, win, b_in, a_re, a_im, bw, cw, tw, dskip, wglu, convw, convb, wco, wout)
    tile = (SEQ_TILE, BATCH_TILE, D_MODEL)
    assert (batch // BATCH_TILE) * (seq // SEQ_TILE) >= X_SLOTS, "the rings assume at least X_SLOTS grid steps"
    return pl.pallas_call(
        _mixer_kernel,
        out_shape=jax.ShapeDtypeStruct(x.shape, jnp.float32),
        grid=(batch // BATCH_TILE, seq // SEQ_TILE),
        in_specs=[pl.BlockSpec(memory_space=pl.ANY)] + [_resident(w.shape) for w in weights],
        out_specs=pl.BlockSpec(memory_space=pl.ANY),
        scratch_shapes=[
            pltpu.VMEM((X_SLOTS, *tile), jnp.float32),
            pltpu.VMEM((2, *tile), jnp.float32),
            pltpu.SemaphoreType.DMA((X_SLOTS,)),
            pltpu.SemaphoreType.DMA((2,)),
            pltpu.VMEM((2, MIX_ROWS, D_MODEL), jnp.bfloat16),
            pltpu.VMEM((CHUNK_ROWS, STATE_LANES), jnp.float32),
            pltpu.VMEM((CHUNK_ROWS, STATE_LANES), jnp.float32),
            pltpu.VMEM((SUBLANES, STATE_LANES), jnp.float32),
            pltpu.VMEM((SUBLANES, STATE_LANES), jnp.float32),
            pltpu.VMEM((MIX_ROWS + (CONV_K - 1) * SUBLANES, D_MODEL), jnp.float32),
        ],
        compiler_params=pltpu.CompilerParams(
            dimension_semantics=("arbitrary", "arbitrary"),
            vmem_limit_bytes=VMEM_LIMIT_BYTES),
        name="mixer",
    )(x, *weights)


def _mlp_kernel(h_ref, gmlp_ref, w1_ref, w2_ref, gfin_ref, o_ref):
    h = h_ref[...]
    hn = _rmsnorm(h, gmlp_ref[...]).astype(jnp.bfloat16)
    acc = h
    for c in range(D_FF // FF_CHUNK):
        a = jnp.maximum(_dot(hn, w1_ref[:, c * FF_CHUNK:(c + 1) * FF_CHUNK]), 0.0)
        acc = acc + _dot((a * a).astype(jnp.bfloat16), w2_ref[c * FF_CHUNK:(c + 1) * FF_CHUNK, :])
    o_ref[...] = _rmsnorm(acc, gfin_ref[...])


def _mlp(h, gmlp, w1, w2, gfin):
    rows, d = h.shape
    assert d == D_MODEL and rows % MLP_ROWS == 0
    tile = pl.BlockSpec((MLP_ROWS, D_MODEL), lambda i: (i, 0))
    weights = (gmlp, w1, w2, gfin)
    in_specs = [tile, _resident(gmlp.shape), _resident(w1.shape), _resident(w2.shape), _resident(gfin.shape)]
    return pl.pallas_call(
        _mlp_kernel,
        out_shape=jax.ShapeDtypeStruct(h.shape, jnp.float32),
        grid=(rows // MLP_ROWS,),
        in_specs=in_specs,
        out_specs=tile,
        compiler_params=pltpu.CompilerParams(
            dimension_semantics=("arbitrary",),
            vmem_limit_bytes=VMEM_LIMIT_BYTES),
        name="mlp",
    )(h, *weights)


def _ssm_weights(lam_re, lam_im, log_dt, b_re, b_im, c_re, c_im):
    n = SSM_CHUNK
    dt = jnp.exp(log_dt)[:, None]

    def a_pow(m):
        m = jnp.asarray(m, jnp.float32)[..., None, None]
        mag = jnp.exp(m * (lam_re * dt))
        return mag * jnp.cos(m * (lam_im * dt)), mag * jnp.sin(m * (lam_im * dt))

    ab_re, ab_im = a_pow(1)
    er, ei = ab_re - 1.0, ab_im
    den = lam_re * lam_re + lam_im * lam_im
    q_re = (er * lam_re + ei * lam_im) / den
    q_im = (ei * lam_re - er * lam_im) / den
    bb_re = q_re[..., None] * b_re - q_im[..., None] * b_im
    bb_im = q_re[..., None] * b_im + q_im[..., None] * b_re

    pr, pi = a_pow(n - 1 - np.arange(n))
    pr, pi = pr[..., None], pi[..., None]
    bw = _group_block_diag(jnp.stack([pr * bb_re - pi * bb_im, pr * bb_im + pi * bb_re]))

    pr, pi = a_pow(np.arange(n + 1))
    pr, pi = pr[..., None], pi[..., None]
    c_re, c_im = c_re.transpose(0, 2, 1), c_im.transpose(0, 2, 1)
    cw_all = _group_block_diag(jnp.stack([c_re * pr - c_im * pi, -(c_re * pi + c_im * pr)]))
    cw_all = cw_all.transpose(0, 2, 1)

    k = jnp.einsum('qrs,qsc->qrc', bw[:, (n - 1) * LANES:, :], cw_all[:, :, :n * LANES],
                   precision=lax.Precision.HIGHEST)
    tw = jnp.concatenate(
        [jnp.concatenate([jnp.zeros((N_BLOCKS, LANES, i * LANES), k.dtype), k[:, :, :(n - i) * LANES]], axis=2)
         for i in range(n)], axis=1)
    an_re, an_im = a_pow(n)
    return an_re, an_im, bw, cw_all[:, :, LANES:], tw


def _group_block_diag(vals):
    parts, steps = vals.shape[:2]
    pair = LANES // SSM_STATE
    x = vals.reshape(parts, steps, N_BLOCKS, GROUPS_PER_BLOCK, SSM_STATE, SSM_GROUP)
    x = x.transpose(2, 1, 3, 5, 0, 4)
    half = np.arange(GROUPS_PER_BLOCK)[:, None] % pair == np.arange(pair)[None, :]
    x = jnp.where(half[None, None, :, None, None, :, None], x[:, :, :, :, :, None, :], 0.0)
    x = x.reshape(N_BLOCKS, steps * LANES, parts, LANES)
    tiled = jnp.concatenate(
        [x[:, :, p, :] for p in range(parts) for _ in range(BLOCK_STATE // LANES)], axis=-1)
    row_pair = (np.arange(steps * LANES) % LANES) // (pair * SSM_GROUP)
    col_tile = (np.arange(parts * BLOCK_STATE) % BLOCK_STATE) // LANES
    return jnp.where(row_pair[:, None] == col_tile[None, :], tiled, 0.0)


def _layer(h, norm_mix_g, w_in, b_in, lam_re, lam_im, log_dt, ssm_b_re, ssm_b_im, ssm_c_re, ssm_c_im,
           ssm_d, w_glu_a, w_glu_b, conv_w, conv_b, w_conv_out, w_out):
    bf16 = jnp.bfloat16
    an_re, an_im, bw, cw, tw = _ssm_weights(lam_re, lam_im, log_dt, ssm_b_re, ssm_b_im, ssm_c_re, ssm_c_im)
    return _mixer(
        h, norm_mix_g.reshape(1, D_MODEL), w_in.astype(bf16), b_in.reshape(1, IN_COLS),
        an_re.reshape(1, STATE_LANES), an_im.reshape(1, STATE_LANES),
        bw.astype(bf16), cw.astype(bf16), tw.astype(bf16),
        ssm_d.reshape(1, SSM_WIDTH), jnp.concatenate([w_glu_a, w_glu_b], axis=1).astype(bf16),
        conv_w, conv_b.reshape(1, D_MODEL), w_conv_out.astype(bf16), w_out.astype(bf16))


def kernel(x, norm_mix_g, w_in, b_in, lam_re, lam_im, log_dt, ssm_b_re, ssm_b_im, ssm_c_re, ssm_c_im,
           ssm_d, w_glu_a, w_glu_b, conv_w, conv_b, w_conv_out, w_out, norm_mlp_g, w_ff1, w_ff2,
           norm_final_g):
    depth = norm_mix_g.shape[0]
    assert depth == 1, "the MLP call applies the final norm, so it must be the last layer"
    batch, seq, d = x.shape
    l = 0
    h = _layer(x, norm_mix_g[l], w_in[l], b_in[l], lam_re[l], lam_im[l], log_dt[l],
               ssm_b_re[l], ssm_b_im[l], ssm_c_re[l], ssm_c_im[l], ssm_d[l],
               w_glu_a[l], w_glu_b[l], conv_w[l], conv_b[l], w_conv_out[l], w_out[l])
    out = _mlp(h.reshape(batch * seq, d), norm_mlp_g[l].reshape(1, d), w_ff1[l].astype(jnp.bfloat16),
               w_ff2[l].astype(jnp.bfloat16), norm_final_g.reshape(1, d))
    return out.reshape(batch, seq, d)
```

```python
import numpy as np
import jax
import jax.numpy as jnp
from jax import lax
from jax.experimental import pallas as pl
from jax.experimental.pallas import tpu as pltpu

LANES = 128
SUBLANES = 8
VMEM_LIMIT_BYTES = 56 * 1024 * 1024

D_MODEL = 1024
SSM_GROUP = 16
SSM_WIDTH = D_MODEL // 2
SSM_GROUPS = SSM_WIDTH // SSM_GROUP
SSM_STATE = 64
CONV_K = 3
D_FF = 4 * D_MODEL
NORM_EPS = 1e-6

BATCH_TILE = SUBLANES
SEQ_TILE = 64
MIX_ROWS = BATCH_TILE * SEQ_TILE
X_SLOTS = 3
SSM_CHUNK = 2
N_CHUNKS = SEQ_TILE // SSM_CHUNK
CHUNK_ROWS = N_CHUNKS * SUBLANES
MLP_ROWS = 1024
FF_CHUNK = 1024

GROUPS_PER_BLOCK = LANES // SSM_GROUP
N_BLOCKS = SSM_GROUPS // GROUPS_PER_BLOCK
BLOCK_STATE = GROUPS_PER_BLOCK * SSM_STATE
STATE_LANES = SSM_GROUPS * SSM_STATE
SCAN_LANES = 512

_O_U = 0
_O_CB = _O_U + SSM_WIDTH
_O_CC = _O_CB + D_MODEL
_O_CV = _O_CC + D_MODEL
_O_GS = _O_CV + D_MODEL
_O_GC = _O_GS + D_MODEL
IN_COLS = _O_GC + D_MODEL


def _rmsnorm(x, g):
    var = jnp.mean(x * x, axis=-1, keepdims=True)
    return x * lax.rsqrt(var + NORM_EPS) * g


def _dot(a, b):
    return jnp.dot(a, b, preferred_element_type=jnp.float32)


def _tile_copies(hbm, buf, sem, step, n_j, slot, to_hbm):
    i, j = lax.div(step, n_j), lax.rem(step, n_j)
    copies = []
    for b in range(BATCH_TILE):
        h = hbm.at[i * BATCH_TILE + b, pl.ds(j * SEQ_TILE, SEQ_TILE), :]
        v = buf.at[slot, :, b, :]
        copies.append(pltpu.make_async_copy(v, h, sem.at[slot]) if to_hbm
                      else pltpu.make_async_copy(h, v, sem.at[slot]))
    return copies


def _mixer_kernel(x_hbm, gmix_ref, win_ref, bin_ref, are_ref, aim_ref, bw_ref, cw_ref, tw_ref, dskip_ref,
                  wglu_ref, convw_ref, convb_ref, wco_ref, wout_ref,
                  o_hbm,
                  xbuf, obuf, in_sem, out_sem, xn16, sr_ref, si_ref, csr_ref, csi_ref, p_ref):
    j = pl.program_id(1)
    n_j = pl.num_programs(1)
    n_steps = pl.num_programs(0) * n_j
    step = pl.program_id(0) * n_j + j
    xslot, xnext = lax.rem(step, X_SLOTS), lax.rem(step + 1, X_SLOTS)
    oslot = lax.rem(step, 2)
    halo = (CONV_K - 1) * SUBLANES

    def x_copies(s):
        return _tile_copies(x_hbm, xbuf, in_sem, s, n_j, lax.rem(s, X_SLOTS), False)

    def out_copies(slot):
        return _tile_copies(o_hbm, obuf, out_sem, step, n_j, slot, True)

    @pl.when(step == 0)
    def _():
        for c in x_copies(0) + x_copies(1):
            c.start()
        for c in x_copies(0):
            c.wait()
        xn16[0] = _rmsnorm(xbuf[0].reshape(MIX_ROWS, D_MODEL), gmix_ref[...]).astype(jnp.bfloat16)

    @pl.when(step + 2 < n_steps)
    def _():
        for c in x_copies(step + 2):
            c.start()

    @pl.when(step + 1 < n_steps)
    def _():
        for c in x_copies(step + 1):
            c.wait()

    @pl.when(step >= 2)
    def _():
        for c in out_copies(oslot):
            c.wait()

    @pl.when(j == 0)
    def _():
        csr_ref[...] = jnp.zeros_like(csr_ref)
        csi_ref[...] = jnp.zeros_like(csi_ref)
        p_ref[0:halo, :] = jnp.zeros((halo, D_MODEL), jnp.float32)

    f32 = jnp.float32
    xn = xn16[oslot]

    u = jnp.dot(xn, win_ref[:, _O_U:_O_CB], preferred_element_type=f32) + bin_ref[:, _O_U:_O_CB]
    u4 = u.reshape(N_CHUNKS, SSM_CHUNK, SUBLANES, SSM_WIDTH)
    ub = []
    for q in range(N_BLOCKS):
        ub.append(jnp.concatenate(
            [u4[:, jj, :, q * LANES:(q + 1) * LANES].reshape(CHUNK_ROWS, LANES) for jj in range(SSM_CHUNK)],
            axis=1).astype(jnp.bfloat16))
        v = jnp.dot(ub[q], bw_ref[q], preferred_element_type=f32)
        sr_ref[:, q * BLOCK_STATE:(q + 1) * BLOCK_STATE] = v[:, :BLOCK_STATE]
        si_ref[:, q * BLOCK_STATE:(q + 1) * BLOCK_STATE] = v[:, BLOCK_STATE:]

    xn16[1 - oslot] = _rmsnorm(xbuf[xnext].reshape(MIX_ROWS, D_MODEL), gmix_ref[...]).astype(jnp.bfloat16)

    for c in range(STATE_LANES // SCAN_LANES):
        sl = slice(c * SCAN_LANES, (c + 1) * SCAN_LANES)
        ar = jnp.broadcast_to(are_ref[:, sl], (SUBLANES, SCAN_LANES))
        ai = jnp.broadcast_to(aim_ref[:, sl], (SUBLANES, SCAN_LANES))
        cr, ci = csr_ref[:, sl], csi_ref[:, sl]
        for k in range(N_CHUNKS):
            rows = slice(k * SUBLANES, (k + 1) * SUBLANES)
            vr, vi = sr_ref[rows, sl], si_ref[rows, sl]
            sr_ref[rows, sl] = cr
            si_ref[rows, sl] = ci
            cr, ci = ar * cr - ai * ci + vr, ar * ci + ai * cr + vi
        csr_ref[:, sl] = cr
        csi_ref[:, sl] = ci

    ys = []
    for q in range(N_BLOCKS):
        sq = jnp.concatenate([sr_ref[:, q * BLOCK_STATE:(q + 1) * BLOCK_STATE],
                              si_ref[:, q * BLOCK_STATE:(q + 1) * BLOCK_STATE]], axis=1)
        yb = jnp.dot(sq.astype(jnp.bfloat16), cw_ref[q], preferred_element_type=f32)
        yb = yb + jnp.dot(ub[q], tw_ref[q], preferred_element_type=f32)
        ys.append(jnp.concatenate(
            [yb[:, jj * LANES:(jj + 1) * LANES].reshape(N_CHUNKS, 1, SUBLANES, LANES)
             for jj in range(SSM_CHUNK)], axis=1).reshape(MIX_ROWS, LANES))
    y_ssm = jnp.concatenate(ys, axis=1) + dskip_ref[...] * u
    z = jax.nn.gelu(y_ssm, approximate=True).astype(jnp.bfloat16)
    ab = jnp.dot(z, wglu_ref[...], preferred_element_type=f32)
    y_a = ab[:, :D_MODEL]
    y_a = y_a + y_a * jnp.tanh(ab[:, D_MODEL:])
    g_ssm = jnp.dot(xn, win_ref[:, _O_GS:_O_GC], preferred_element_type=f32) + bin_ref[:, _O_GS:_O_GC]
    merged = y_a + y_a * jnp.tanh(g_ssm)

    c_c = jnp.dot(xn, win_ref[:, _O_CC:_O_CV], preferred_element_type=f32) + bin_ref[:, _O_CC:_O_CV]
    c_v = jnp.dot(xn, win_ref[:, _O_CV:_O_GS], preferred_element_type=f32) + bin_ref[:, _O_CV:_O_GS]
    p_ref[halo:halo + MIX_ROWS, :] = c_c * c_v
    conv = convb_ref[...]
    for k in range(CONV_K):
        conv = conv + convw_ref[k:k + 1, :] * p_ref[k * SUBLANES:k * SUBLANES + MIX_ROWS, :]
    p_ref[0:halo, :] = p_ref[MIX_ROWS:MIX_ROWS + halo, :]
    c_b = jnp.dot(xn, win_ref[:, _O_CB:_O_CC], preferred_element_type=f32) + bin_ref[:, _O_CB:_O_CC]
    y_b = jnp.dot((c_b * conv).astype(jnp.bfloat16), wco_ref[...], preferred_element_type=f32)
    g_conv = jnp.dot(xn, win_ref[:, _O_GC:IN_COLS], preferred_element_type=f32) + bin_ref[:, _O_GC:IN_COLS]
    merged = merged + (y_b + y_b * jnp.tanh(g_conv))

    mix = jnp.dot(merged.astype(jnp.bfloat16), wout_ref[...], preferred_element_type=f32)
    obuf[oslot] = (xbuf[xslot].reshape(MIX_ROWS, D_MODEL) + mix).reshape(SEQ_TILE, BATCH_TILE, D_MODEL)

    for c in out_copies(oslot):
        c.start()

    @pl.when(step == n_steps - 1)
    def _():
        for c in out_copies(oslot) + out_copies(1 - oslot):
            c.wait()


def _resident(shape):
    return pl.BlockSpec(shape, lambda *_: (0,) * len(shape), pipeline_mode=pl.Buffered(1))


def _mixer(x, gmix, win, b_in, a_re, a_im, bw, cw, tw, dskip, wglu, convw, convb, wco, wout):
    batch, seq, d = x.shape
    assert d == D_MODEL and batch % BATCH_TILE == 0 and seq % SEQ_TILE == 0
    weights = (gmix, win, b_in, a_re, a_im, bw, cw, tw, dskip, wglu, convw, convb, wco, wout)
    tile = (SEQ_TILE, BATCH_TILE, D_MODEL)
    assert (batch // BATCH_TILE) * (seq // SEQ_TILE) >= X_SLOTS, "the rings assume at least X_SLOTS grid steps"
    return pl.pallas_call(
        _mixer_kernel,
        out_shape=jax.ShapeDtypeStruct(x.shape, jnp.float32),
        grid=(batch // BATCH_TILE, seq // SEQ_TILE),
        in_specs=[pl.BlockSpec(memory_space=pl.ANY)] + [_resident(w.shape) for w in weights],
        out_specs=pl.BlockSpec(memory_space=pl.ANY),
        scratch_shapes=[
            pltpu.VMEM((X_SLOTS, *tile), jnp.float32),
            pltpu.VMEM((2, *tile), jnp.float32),
            pltpu.SemaphoreType.DMA((X_SLOTS,)),
            pltpu.SemaphoreType.DMA((2,)),
            pltpu.VMEM((2, MIX_ROWS, D_MODEL), jnp.bfloat16),
            pltpu.VMEM((CHUNK_ROWS, STATE_LANES), jnp.float32),
            pltpu.VMEM((CHUNK_ROWS, STATE_LANES), jnp.float32),
            pltpu.VMEM((SUBLANES, STATE_LANES), jnp.float32),
            pltpu.VMEM((SUBLANES, STATE_LANES), jnp.float32),
            pltpu.VMEM((MIX_ROWS + (CONV_K - 1) * SUBLANES, D_MODEL), jnp.float32),
        ],
        compiler_params=pltpu.CompilerParams(
            dimension_semantics=("arbitrary", "arbitrary"),
            vmem_limit_bytes=VMEM_LIMIT_BYTES),
        name="mixer",
    )(x, *weights)


def _mlp_kernel(h_ref, gmlp_ref, w1_ref, w2_ref, gfin_ref, o_ref):
    h = h_ref[...]
    hn = _rmsnorm(h, gmlp_ref[...]).astype(jnp.bfloat16)
    acc = h
    for c in range(D_FF // FF_CHUNK):
        a = jnp.maximum(_dot(hn, w1_ref[:, c * FF_CHUNK:(c + 1) * FF_CHUNK]), 0.0)
        acc = acc + _dot((a * a).astype(jnp.bfloat16), w2_ref[c * FF_CHUNK:(c + 1) * FF_CHUNK, :])
    o_ref[...] = _rmsnorm(acc, gfin_ref[...])


def _mlp(h, gmlp, w1, w2, gfin):
    rows, d = h.shape
    assert d == D_MODEL and rows % MLP_ROWS == 0
    tile = pl.BlockSpec((MLP_ROWS, D_MODEL), lambda i: (i, 0))
    weights = (gmlp, w1, w2, gfin)
    in_specs = [tile, _resident(gmlp.shape), _resident(w1.shape), _resident(w2.shape), _resident(gfin.shape)]
    return pl.pallas_call(
        _mlp_kernel,
        out_shape=jax.ShapeDtypeStruct(h.shape, jnp.float32),
        grid=(rows // MLP_ROWS,),
        in_specs=in_specs,
        out_specs=tile,
        compiler_params=pltpu.CompilerParams(
            dimension_semantics=("arbitrary",),
            vmem_limit_bytes=VMEM_LIMIT_BYTES),
        name="mlp",
    )(h, *weights)


def _ssm_weights(lam_re, lam_im, log_dt, b_re, b_im, c_re, c_im):
    n = SSM_CHUNK
    dt = jnp.exp(log_dt)[:, None]

    def a_pow(m):
        m = jnp.asarray(m, jnp.float32)[..., None, None]
        mag = jnp.exp(m * (lam_re * dt))
        return mag * jnp.cos(m * (lam_im * dt)), mag * jnp.sin(m * (lam_im * dt))

    ab_re, ab_im = a_pow(1)
    er, ei = ab_re - 1.0, ab_im
    den = lam_re * lam_re + lam_im * lam_im
    q_re = (er * lam_re + ei * lam_im) / den
    q_im = (ei * lam_re - er * lam_im) / den
    bb_re = q_re[..., None] * b_re - q_im[..., None] * b_im
    bb_im = q_re[..., None] * b_im + q_im[..., None] * b_re

    pr, pi = a_pow(n - 1 - np.arange(n))
    pr, pi = pr[..., None], pi[..., None]
    bw = _group_block_diag(jnp.stack([pr * bb_re - pi * bb_im, pr * bb_im + pi * bb_re]))

    pr, pi = a_pow(np.arange(n + 1))
    pr, pi = pr[..., None], pi[..., None]
    c_re, c_im = c_re.transpose(0, 2, 1), c_im.transpose(0, 2, 1)
    cw_all = _group_block_diag(jnp.stack([c_re * pr - c_im * pi, -(c_re * pi + c_im * pr)]))
    cw_all = cw_all.transpose(0, 2, 1)

    k = jnp.einsum('qrs,qsc->qrc', bw[:, (n - 1) * LANES:, :], cw_all[:, :, :n * LANES],
                   precision=lax.Precision.HIGHEST)
    tw = jnp.concatenate(
        [jnp.concatenate([jnp.zeros((N_BLOCKS, LANES, i * LANES), k.dtype), k[:, :, :(n - i) * LANES]], axis=2)
         for i in range(n)], axis=1)
    an_re, an_im = a_pow(n)
    return an_re, an_im, bw, cw_all[:, :, LANES:], tw


def _group_block_diag(vals):
    parts, steps = vals.shape[:2]
    pair = LANES // SSM_STATE
    x = vals.reshape(parts, steps, N_BLOCKS, GROUPS_PER_BLOCK, SSM_STATE, SSM_GROUP)
    x = x.transpose(2, 1, 3, 5, 0, 4)
    half = np.arange(GROUPS_PER_BLOCK)[:, None] % pair == np.arange(pair)[None, :]
    x = jnp.where(half[None, None, :, None, None, :, None], x[:, :, :, :, :, None, :], 0.0)
    x = x.reshape(N_BLOCKS, steps * LANES, parts, LANES)
    tiled = jnp.concatenate(
        [x[:, :, p, :] for p in range(parts) for _ in range(BLOCK_STATE // LANES)], axis=-1)
    row_pair = (np.arange(steps * LANES) % LANES) // (pair * SSM_GROUP)
    col_tile = (np.arange(parts * BLOCK_STATE) % BLOCK_STATE) // LANES
    return jnp.where(row_pair[:, None] == col_tile[None, :], tiled, 0.0)


def _layer(h, norm_mix_g, w_in, b_in, lam_re, lam_im, log_dt, ssm_b_re, ssm_b_im, ssm_c_re, ssm_c_im,
           ssm_d, w_glu_a, w_glu_b, conv_w, conv_b, w_conv_out, w_out):
    bf16 = jnp.bfloat16
    an_re, an_im, bw, cw, tw = _ssm_weights(lam_re, lam_im, log_dt, ssm_b_re, ssm_b_im, ssm_c_re, ssm_c_im)
    gate_half = np.where(np.arange(IN_COLS) >= _O_GS, 0.5, 1.0).astype(np.float32)[None, :]
    return _mixer(
        h, norm_mix_g.reshape(1, D_MODEL), (w_in * gate_half).astype(bf16), b_in.reshape(1, IN_COLS) * gate_half,
        an_re.reshape(1, STATE_LANES), an_im.reshape(1, STATE_LANES),
        bw.astype(bf16), cw.astype(bf16), tw.astype(bf16),
        ssm_d.reshape(1, SSM_WIDTH), jnp.concatenate([0.25 * w_glu_a, 0.5 * w_glu_b], axis=1).astype(bf16),
        conv_w, conv_b.reshape(1, D_MODEL), (0.5 * w_conv_out).astype(bf16), w_out.astype(bf16))


def kernel(x, norm_mix_g, w_in, b_in, lam_re, lam_im, log_dt, ssm_b_re, ssm_b_im, ssm_c_re, ssm_c_im,
           ssm_d, w_glu_a, w_glu_b, conv_w, conv_b, w_conv_out, w_out, norm_mlp_g, w_ff1, w_ff2,
           norm_final_g):
    depth = norm_mix_g.shape[0]
    assert depth == 1, "the MLP call applies the final norm, so it must be the last layer"
    batch, seq, d = x.shape
    l = 0
    h = _layer(x, norm_mix_g[l], w_in[l], b_in[l], lam_re[l], lam_im[l], log_dt[l],
               ssm_b_re[l], ssm_b_im[l], ssm_c_re[l], ssm_c_im[l], ssm_d[l],
               w_glu_a[l], w_glu_b[l], conv_w[l], conv_b[l], w_conv_out[l], w_out[l])
    out = _mlp(h.reshape(batch * seq, d), norm_mlp_g[l].reshape(1, d), w_ff1[l].astype(jnp.bfloat16),
               w_ff2[l].astype(jnp.bfloat16), norm_final_g.reshape(1, d))
    return out.reshape(batch, seq, d)
```

```python
import numpy as np
import jax
import jax.numpy as jnp
from jax import lax
from jax.experimental import pallas as pl
from jax.experimental.pallas import tpu as pltpu

LANES = 128
SUBLANES = 8
VMEM_LIMIT_BYTES = 56 * 1024 * 1024

D_MODEL = 1024
SSM_GROUP = 16
SSM_WIDTH = D_MODEL // 2
SSM_GROUPS = SSM_WIDTH // SSM_GROUP
SSM_STATE = 64
CONV_K = 3
D_FF = 4 * D_MODEL
NORM_EPS = 1e-6

BATCH_TILE = SUBLANES
SEQ_TILE = 64
MIX_ROWS = BATCH_TILE * SEQ_TILE
X_SLOTS = 3
SSM_CHUNK = 2
N_CHUNKS = SEQ_TILE // SSM_CHUNK
CHUNK_ROWS = N_CHUNKS * SUBLANES
MLP_ROWS = 1024
FF_CHUNK = 1024

GROUPS_PER_BLOCK = LANES // SSM_GROUP
N_BLOCKS = SSM_GROUPS // GROUPS_PER_BLOCK
BLOCK_STATE = GROUPS_PER_BLOCK * SSM_STATE
STATE_LANES = SSM_GROUPS * SSM_STATE
SCAN_LANES = 512

_O_U = 0
_O_CB = _O_U + SSM_WIDTH
_O_CC = _O_CB + D_MODEL
_O_CV = _O_CC + D_MODEL
_O_GS = _O_CV + D_MODEL
_O_GC = _O_GS + D_MODEL
IN_COLS = _O_GC + D_MODEL


def _rmsnorm(x, g):
    var = jnp.mean(x * x, axis=-1, keepdims=True)
    return x * lax.rsqrt(var + NORM_EPS) * g


def _dot(a, b):
    return jnp.dot(a, b, preferred_element_type=jnp.float32)


def _tile_copies(hbm, buf, sem, step, n_j, slot, to_hbm):
    i, j = lax.div(step, n_j), lax.rem(step, n_j)
    copies = []
    for b in range(BATCH_TILE):
        h = hbm.at[i * BATCH_TILE + b, pl.ds(j * SEQ_TILE, SEQ_TILE), :]
        v = buf.at[slot, :, b, :]
        copies.append(pltpu.make_async_copy(v, h, sem.at[slot]) if to_hbm
                      else pltpu.make_async_copy(h, v, sem.at[slot]))
    return copies


def _mixer_kernel(x_hbm, gmix_ref, win_ref, bin_ref, are_ref, aim_ref, bw_ref, cw_ref, tw_ref, dskip_ref,
                  wglu_ref, convw_ref, convb_ref, wco_ref, wout_ref,
                  o_hbm,
                  xbuf, obuf, in_sem, out_sem, xn16, u_ref, sr_ref, si_ref, csr_ref, csi_ref, p_ref):
    j = pl.program_id(1)
    n_j = pl.num_programs(1)
    n_steps = pl.num_programs(0) * n_j
    step = pl.program_id(0) * n_j + j
    xslot, xnext = lax.rem(step, X_SLOTS), lax.rem(step + 1, X_SLOTS)
    oslot = lax.rem(step, 2)
    halo = (CONV_K - 1) * SUBLANES

    def x_copies(s):
        return _tile_copies(x_hbm, xbuf, in_sem, s, n_j, lax.rem(s, X_SLOTS), False)

    def out_copies(slot):
        return _tile_copies(o_hbm, obuf, out_sem, step, n_j, slot, True)

    @pl.when(step == 0)
    def _():
        for c in x_copies(0) + x_copies(1):
            c.start()
        for c in x_copies(0):
            c.wait()
        xn0 = _rmsnorm(xbuf[0].reshape(MIX_ROWS, D_MODEL), gmix_ref[...]).astype(jnp.bfloat16)
        xn16[0] = xn0
        u_ref[...] = (jnp.dot(xn0, win_ref[:, _O_U:_O_CB], preferred_element_type=jnp.float32)
                      + bin_ref[:, _O_U:_O_CB])

    @pl.when(step + 2 < n_steps)
    def _():
        for c in x_copies(step + 2):
            c.start()

    @pl.when(step + 1 < n_steps)
    def _():
        for c in x_copies(step + 1):
            c.wait()

    @pl.when(step >= 2)
    def _():
        for c in out_copies(oslot):
            c.wait()

    @pl.when(j == 0)
    def _():
        csr_ref[...] = jnp.zeros_like(csr_ref)
        csi_ref[...] = jnp.zeros_like(csi_ref)
        p_ref[0:halo, :] = jnp.zeros((halo, D_MODEL), jnp.float32)

    f32 = jnp.float32
    xn = xn16[oslot]

    u = u_ref[...]
    c_c = jnp.dot(xn, win_ref[:, _O_CC:_O_CV], preferred_element_type=f32) + bin_ref[:, _O_CC:_O_CV]
    u4 = u.reshape(N_CHUNKS, SSM_CHUNK, SUBLANES, SSM_WIDTH)
    ub = []
    for q in range(N_BLOCKS):
        ub.append(jnp.concatenate(
            [u4[:, jj, :, q * LANES:(q + 1) * LANES].reshape(CHUNK_ROWS, LANES) for jj in range(SSM_CHUNK)],
            axis=1).astype(jnp.bfloat16))
        v = jnp.dot(ub[q], bw_ref[q], preferred_element_type=f32)
        sr_ref[:, q * BLOCK_STATE:(q + 1) * BLOCK_STATE] = v[:, :BLOCK_STATE]
        si_ref[:, q * BLOCK_STATE:(q + 1) * BLOCK_STATE] = v[:, BLOCK_STATE:]
    c_v = jnp.dot(xn, win_ref[:, _O_CV:_O_GS], preferred_element_type=f32) + bin_ref[:, _O_CV:_O_GS]
    p_ref[halo:halo + MIX_ROWS, :] = c_c * c_v

    xn_next = _rmsnorm(xbuf[xnext].reshape(MIX_ROWS, D_MODEL), gmix_ref[...]).astype(jnp.bfloat16)
    xn16[1 - oslot] = xn_next

    for c in range(STATE_LANES // SCAN_LANES):
        sl = slice(c * SCAN_LANES, (c + 1) * SCAN_LANES)
        ar = jnp.broadcast_to(are_ref[:, sl], (SUBLANES, SCAN_LANES))
        ai = jnp.broadcast_to(aim_ref[:, sl], (SUBLANES, SCAN_LANES))
        cr, ci = csr_ref[:, sl], csi_ref[:, sl]
        for k in range(N_CHUNKS):
            rows = slice(k * SUBLANES, (k + 1) * SUBLANES)
            vr, vi = sr_ref[rows, sl], si_ref[rows, sl]
            sr_ref[rows, sl] = cr
            si_ref[rows, sl] = ci
            cr, ci = ar * cr - ai * ci + vr, ar * ci + ai * cr + vi
        csr_ref[:, sl] = cr
        csi_ref[:, sl] = ci

    conv = convb_ref[...]
    for k in range(CONV_K):
        conv = conv + convw_ref[k:k + 1, :] * p_ref[k * SUBLANES:k * SUBLANES + MIX_ROWS, :]
    p_ref[0:halo, :] = p_ref[MIX_ROWS:MIX_ROWS + halo, :]
    c_b = jnp.dot(xn, win_ref[:, _O_CB:_O_CC], preferred_element_type=f32) + bin_ref[:, _O_CB:_O_CC]
    cbc = (c_b * conv).astype(jnp.bfloat16)

    ys = []
    for q in range(N_BLOCKS):
        sq = jnp.concatenate([sr_ref[:, q * BLOCK_STATE:(q + 1) * BLOCK_STATE],
                              si_ref[:, q * BLOCK_STATE:(q + 1) * BLOCK_STATE]], axis=1)
        yb = jnp.dot(sq.astype(jnp.bfloat16), cw_ref[q], preferred_element_type=f32)
        yb = yb + jnp.dot(ub[q], tw_ref[q], preferred_element_type=f32)
        ys.append(jnp.concatenate(
            [yb[:, jj * LANES:(jj + 1) * LANES].reshape(N_CHUNKS, 1, SUBLANES, LANES)
             for jj in range(SSM_CHUNK)], axis=1).reshape(MIX_ROWS, LANES))
    y_ssm = jnp.concatenate(ys, axis=1) + dskip_ref[...] * u
    z = jax.nn.gelu(y_ssm, approximate=True).astype(jnp.bfloat16)
    g_ssm = jnp.dot(xn, win_ref[:, _O_GS:_O_GC], preferred_element_type=f32) + bin_ref[:, _O_GS:_O_GC]
    ab = jnp.dot(z, wglu_ref[...], preferred_element_type=f32)
    y_a = ab[:, :D_MODEL]
    y_a = y_a + y_a * jnp.tanh(ab[:, D_MODEL:])
    merged = y_a + y_a * jnp.tanh(g_ssm)
    y_b = jnp.dot(cbc, wco_ref[...], preferred_element_type=f32)
    g_conv = jnp.dot(xn, win_ref[:, _O_GC:IN_COLS], preferred_element_type=f32) + bin_ref[:, _O_GC:IN_COLS]
    merged = merged + (y_b + y_b * jnp.tanh(g_conv))

    mix = jnp.dot(merged.astype(jnp.bfloat16), wout_ref[...], preferred_element_type=f32)
    u_ref[...] = jnp.dot(xn_next, win_ref[:, _O_U:_O_CB], preferred_element_type=f32) + bin_ref[:, _O_U:_O_CB]
    obuf[oslot] = (xbuf[xslot].reshape(MIX_ROWS, D_MODEL) + mix).reshape(SEQ_TILE, BATCH_TILE, D_MODEL)

    for c in out_copies(oslot):
        c.start()

    @pl.when(step == n_steps - 1)
    def _():
        for c in out_copies(oslot) + out_copies(1 - oslot):
            c.wait()


def _resident(shape):
    return pl.BlockSpec(shape, lambda *_: (0,) * len(shape), pipeline_mode=pl.Buffered(1))


def _mixer(x, gmix, win, b_in, a_re, a_im, bw, cw, tw, dskip, wglu, convw, convb, wco, wout):
    batch, seq, d = x.shape
    assert d == D_MODEL and batch % BATCH_TILE == 0 and seq % SEQ_TILE == 0
    weights = (gmix, win, b_in, a_re, a_im, bw, cw, tw, dskip, wglu, convw, convb, wco, wout)
    tile = (SEQ_TILE, BATCH_TILE, D_MODEL)
    assert (batch // BATCH_TILE) * (seq // SEQ_TILE) >= X_SLOTS, "the rings assume at least X_SLOTS grid steps"
    return pl.pallas_call(
        _mixer_kernel,
        out_shape=jax.ShapeDtypeStruct(x.shape, jnp.float32),
        grid=(batch // BATCH_TILE, seq // SEQ_TILE),
        in_specs=[pl.BlockSpec(memory_space=pl.ANY)] + [_resident(w.shape) for w in weights],
        out_specs=pl.BlockSpec(memory_space=pl.ANY),
        scratch_shapes=[
            pltpu.VMEM((X_SLOTS, *tile), jnp.float32),
            pltpu.VMEM((2, *tile), jnp.float32),
            pltpu.SemaphoreType.DMA((X_SLOTS,)),
            pltpu.SemaphoreType.DMA((2,)),
            pltpu.VMEM((2, MIX_ROWS, D_MODEL), jnp.bfloat16),
            pltpu.VMEM((MIX_ROWS, SSM_WIDTH), jnp.float32),
            pltpu.VMEM((CHUNK_ROWS, STATE_LANES), jnp.float32),
            pltpu.VMEM((CHUNK_ROWS, STATE_LANES), jnp.float32),
            pltpu.VMEM((SUBLANES, STATE_LANES), jnp.float32),
            pltpu.VMEM((SUBLANES, STATE_LANES), jnp.float32),
            pltpu.VMEM((MIX_ROWS + (CONV_K - 1) * SUBLANES, D_MODEL), jnp.float32),
        ],
        compiler_params=pltpu.CompilerParams(
            dimension_semantics=("arbitrary", "arbitrary"),
            vmem_limit_bytes=VMEM_LIMIT_BYTES),
        name="mixer",
    )(x, *weights)


def _mlp_kernel(h_ref, gmlp_ref, w1_ref, w2_ref, gfin_ref, o_ref):
    h = h_ref[...]
    hn = _rmsnorm(h, gmlp_ref[...]).astype(jnp.bfloat16)
    acc = h
    for c in range(D_FF // FF_CHUNK):
        a = jnp.maximum(_dot(hn, w1_ref[:, c * FF_CHUNK:(c + 1) * FF_CHUNK]), 0.0)
        acc = acc + _dot((a * a).astype(jnp.bfloat16), w2_ref[c * FF_CHUNK:(c + 1) * FF_CHUNK, :])
    o_ref[...] = _rmsnorm(acc, gfin_ref[...])


def _mlp(h, gmlp, w1, w2, gfin):
    rows, d = h.shape
    assert d == D_MODEL and rows % MLP_ROWS == 0
    tile = pl.BlockSpec((MLP_ROWS, D_MODEL), lambda i: (i, 0))
    weights = (gmlp, w1, w2, gfin)
    in_specs = [tile, _resident(gmlp.shape), _resident(w1.shape), _resident(w2.shape), _resident(gfin.shape)]
    return pl.pallas_call(
        _mlp_kernel,
        out_shape=jax.ShapeDtypeStruct(h.shape, jnp.float32),
        grid=(rows // MLP_ROWS,),
        in_specs=in_specs,
        out_specs=tile,
        compiler_params=pltpu.CompilerParams(
            dimension_semantics=("arbitrary",),
            vmem_limit_bytes=VMEM_LIMIT_BYTES),
        name="mlp",
    )(h, *weights)


def _ssm_weights(lam_re, lam_im, log_dt, b_re, b_im, c_re, c_im):
    n = SSM_CHUNK
    dt = jnp.exp(log_dt)[:, None]

    def a_pow(m):
        m = jnp.asarray(m, jnp.float32)[..., None, None]
        mag = jnp.exp(m * (lam_re * dt))
        return mag * jnp.cos(m * (lam_im * dt)), mag * jnp.sin(m * (lam_im * dt))

    ab_re, ab_im = a_pow(1)
    er, ei = ab_re - 1.0, ab_im
    den = lam_re * lam_re + lam_im * lam_im
    q_re = (er * lam_re + ei * lam_im) / den
    q_im = (ei * lam_re - er * lam_im) / den
    bb_re = q_re[..., None] * b_re - q_im[..., None] * b_im
    bb_im = q_re[..., None] * b_im + q_im[..., None] * b_re

    pr, pi = a_pow(n - 1 - np.arange(n))
    pr, pi = pr[..., None], pi[..., None]
    bw = _group_block_diag(jnp.stack([pr * bb_re - pi * bb_im, pr * bb_im + pi * bb_re]))

    pr, pi = a_pow(np.arange(n + 1))
    pr, pi = pr[..., None], pi[..., None]
    c_re, c_im = c_re.transpose(0, 2, 1), c_im.transpose(0, 2, 1)
    cw_all = _group_block_diag(jnp.stack([c_re * pr - c_im * pi, -(c_re * pi + c_im * pr)]))
    cw_all = cw_all.transpose(0, 2, 1)

    k = jnp.einsum('qrs,qsc->qrc', bw[:, (n - 1) * LANES:, :], cw_all[:, :, :n * LANES],
                   precision=lax.Precision.HIGHEST)
    tw = jnp.concatenate(
        [jnp.concatenate([jnp.zeros((N_BLOCKS, LANES, i * LANES), k.dtype), k[:, :, :(n - i) * LANES]], axis=2)
         for i in range(n)], axis=1)
    an_re, an_im = a_pow(n)
    return an_re, an_im, bw, cw_all[:, :, LANES:], tw


def _group_block_diag(vals):
    parts, steps = vals.shape[:2]
    pair = LANES // SSM_STATE
    x = vals.reshape(parts, steps, N_BLOCKS, GROUPS_PER_BLOCK, SSM_STATE, SSM_GROUP)
    x = x.transpose(2, 1, 3, 5, 0, 4)
    half = np.arange(GROUPS_PER_BLOCK)[:, None] % pair == np.arange(pair)[None, :]
    x = jnp.where(half[None, None, :, None, None, :, None], x[:, :, :, :, :, None, :], 0.0)
    x = x.reshape(N_BLOCKS, steps * LANES, parts, LANES)
    tiled = jnp.concatenate(
        [x[:, :, p, :] for p in range(parts) for _ in range(BLOCK_STATE // LANES)], axis=-1)
    row_pair = (np.arange(steps * LANES) % LANES) // (pair * SSM_GROUP)
    col_tile = (np.arange(parts * BLOCK_STATE) % BLOCK_STATE) // LANES
    return jnp.where(row_pair[:, None] == col_tile[None, :], tiled, 0.0)


def _layer(h, norm_mix_g, w_in, b_in, lam_re, lam_im, log_dt, ssm_b_re, ssm_b_im, ssm_c_re, ssm_c_im,
           ssm_d, w_glu_a, w_glu_b, conv_w, conv_b, w_conv_out, w_out):
    bf16 = jnp.bfloat16
    an_re, an_im, bw, cw, tw = _ssm_weights(lam_re, lam_im, log_dt, ssm_b_re, ssm_b_im, ssm_c_re, ssm_c_im)
    gate_half = np.where(np.arange(IN_COLS) >= _O_GS, 0.5, 1.0).astype(np.float32)[None, :]
    return _mixer(
        h, norm_mix_g.reshape(1, D_MODEL), (w_in * gate_half).astype(bf16), b_in.reshape(1, IN_COLS) * gate_half,
        an_re.reshape(1, STATE_LANES), an_im.reshape(1, STATE_LANES),
        bw.astype(bf16), cw.astype(bf16), tw.astype(bf16),
        ssm_d.reshape(1, SSM_WIDTH), jnp.concatenate([0.25 * w_glu_a, 0.5 * w_glu_b], axis=1).astype(bf16),
        conv_w, conv_b.reshape(1, D_MODEL), (0.5 * w_conv_out).astype(bf16), w_out.astype(bf16))


def kernel(x, norm_mix_g, w_in, b_in, lam_re, lam_im, log_dt, ssm_b_re, ssm_b_im, ssm_c_re, ssm_c_im,
           ssm_d, w_glu_a, w_glu_b, conv_w, conv_b, w_conv_out, w_out, norm_mlp_g, w_ff1, w_ff2,
           norm_final_g):
    depth = norm_mix_g.shape[0]
    assert depth == 1, "the MLP call applies the final norm, so it must be the last layer"
    batch, seq, d = x.shape
    l = 0
    h = _layer(x, norm_mix_g[l], w_in[l], b_in[l], lam_re[l], lam_im[l], log_dt[l],
               ssm_b_re[l], ssm_b_im[l], ssm_c_re[l], ssm_c_im[l], ssm_d[l],
               w_glu_a[l], w_glu_b[l], conv_w[l], conv_b[l], w_conv_out[l], w_out[l])
    out = _mlp(h.reshape(batch * seq, d), norm_mlp_g[l].reshape(1, d), w_ff1[l].astype(jnp.bfloat16),
               w_ff2[l].astype(jnp.bfloat16), norm_final_g.reshape(1, d))
    return out.reshape(batch, seq, d)
```

```python
import numpy as np
import jax
import jax.numpy as jnp
from jax import lax
from jax.experimental import pallas as pl
from jax.experimental.pallas import tpu as pltpu

LANES = 128
SUBLANES = 8
VMEM_LIMIT_BYTES = 56 * 1024 * 1024

D_MODEL = 1024
SSM_GROUP = 16
SSM_WIDTH = D_MODEL // 2
SSM_GROUPS = SSM_WIDTH // SSM_GROUP
SSM_STATE = 64
CONV_K = 3
D_FF = 4 * D_MODEL
NORM_EPS = 1e-6

BATCH_TILE = SUBLANES
SEQ_TILE = 64
MIX_ROWS = BATCH_TILE * SEQ_TILE
X_SLOTS = 3
SSM_CHUNK = 2
N_CHUNKS = SEQ_TILE // SSM_CHUNK
CHUNK_ROWS = N_CHUNKS * SUBLANES
MLP_ROWS = 1024
FF_CHUNK = 1024

GROUPS_PER_BLOCK = LANES // SSM_GROUP
N_BLOCKS = SSM_GROUPS // GROUPS_PER_BLOCK
BLOCK_STATE = GROUPS_PER_BLOCK * SSM_STATE
STATE_LANES = SSM_GROUPS * SSM_STATE
SCAN_LANES = 512

_O_U = 0
_O_CB = _O_U + SSM_WIDTH
_O_CC = _O_CB + D_MODEL
_O_CV = _O_CC + D_MODEL
_O_GS = _O_CV + D_MODEL
_O_GC = _O_GS + D_MODEL
IN_COLS = _O_GC + D_MODEL


def _rmsnorm(x, g):
    var = jnp.mean(x * x, axis=-1, keepdims=True)
    return x * lax.rsqrt(var + NORM_EPS) * g


def _dot(a, b):
    return jnp.dot(a, b, preferred_element_type=jnp.float32)


def _tile_copies(hbm, buf, sem, step, n_j, slot, to_hbm):
    i, j = lax.div(step, n_j), lax.rem(step, n_j)
    copies = []
    for b in range(BATCH_TILE):
        h = hbm.at[i * BATCH_TILE + b, pl.ds(j * SEQ_TILE, SEQ_TILE), :]
        v = buf.at[slot, :, b, :]
        copies.append(pltpu.make_async_copy(v, h, sem.at[slot]) if to_hbm
                      else pltpu.make_async_copy(h, v, sem.at[slot]))
    return copies


def _mixer_kernel(x_hbm, gmix_ref, win_ref, bin_ref, are_ref, aim_ref, bw_ref, cw_ref, tw_ref, dskip_ref,
                  wglu_ref, convw_ref, convb_ref, wco_ref, wout_ref,
                  o_hbm,
                  xbuf, obuf, in_sem, out_sem, xn16, u_ref, sr_ref, si_ref, csr_ref, csi_ref, p_ref):
    j = pl.program_id(1)
    n_j = pl.num_programs(1)
    n_steps = pl.num_programs(0) * n_j
    step = pl.program_id(0) * n_j + j
    xslot, xnext = lax.rem(step, X_SLOTS), lax.rem(step + 1, X_SLOTS)
    oslot = lax.rem(step, 2)
    halo = (CONV_K - 1) * SUBLANES

    def x_copies(s):
        return _tile_copies(x_hbm, xbuf, in_sem, s, n_j, lax.rem(s, X_SLOTS), False)

    def out_copies(slot):
        return _tile_copies(o_hbm, obuf, out_sem, step, n_j, slot, True)

    @pl.when(step == 0)
    def _():
        for c in x_copies(0) + x_copies(1):
            c.start()
        for c in x_copies(0):
            c.wait()
        xn0 = _rmsnorm(xbuf[0].reshape(MIX_ROWS, D_MODEL), gmix_ref[...]).astype(jnp.bfloat16)
        xn16[0] = xn0
        u_ref[...] = (jnp.dot(xn0, win_ref[:, _O_U:_O_CB], preferred_element_type=jnp.float32)
                      + bin_ref[:, _O_U:_O_CB])

    @pl.when(step + 2 < n_steps)
    def _():
        for c in x_copies(step + 2):
            c.start()

    @pl.when(step + 1 < n_steps)
    def _():
        for c in x_copies(step + 1):
            c.wait()

    @pl.when(step >= 2)
    def _():
        for c in out_copies(oslot):
            c.wait()

    @pl.when(j == 0)
    def _():
        csr_ref[...] = jnp.zeros_like(csr_ref)
        csi_ref[...] = jnp.zeros_like(csi_ref)
        p_ref[0:halo, :] = jnp.zeros((halo, D_MODEL), jnp.float32)

    f32 = jnp.float32
    xn = xn16[oslot]

    u = u_ref[...]
    c_c = jnp.dot(xn, win_ref[:, _O_CC:_O_CV], preferred_element_type=f32) + bin_ref[:, _O_CC:_O_CV]
    u4 = u.reshape(N_CHUNKS, SSM_CHUNK, SUBLANES, SSM_WIDTH)
    piece = D_MODEL // N_BLOCKS
    ub = []
    for q in range(N_BLOCKS):
        ub.append(jnp.concatenate(
            [u4[:, jj, :, q * LANES:(q + 1) * LANES].reshape(CHUNK_ROWS, LANES) for jj in range(SSM_CHUNK)],
            axis=1).astype(jnp.bfloat16))
        v = jnp.dot(ub[q], bw_ref[q], preferred_element_type=f32)
        sr_ref[:, q * BLOCK_STATE:(q + 1) * BLOCK_STATE] = v[:, :BLOCK_STATE]
        si_ref[:, q * BLOCK_STATE:(q + 1) * BLOCK_STATE] = v[:, BLOCK_STATE:]
        cols = slice(_O_CV + q * piece, _O_CV + (q + 1) * piece)
        c_v = jnp.dot(xn, win_ref[:, cols], preferred_element_type=f32) + bin_ref[:, cols]
        p_ref[halo:halo + MIX_ROWS, q * piece:(q + 1) * piece] = c_c[:, q * piece:(q + 1) * piece] * c_v

    xn_next = _rmsnorm(xbuf[xnext].reshape(MIX_ROWS, D_MODEL), gmix_ref[...]).astype(jnp.bfloat16)
    xn16[1 - oslot] = xn_next

    for c in range(STATE_LANES // SCAN_LANES):
        sl = slice(c * SCAN_LANES, (c + 1) * SCAN_LANES)
        ar = jnp.broadcast_to(are_ref[:, sl], (SUBLANES, SCAN_LANES))
        ai = jnp.broadcast_to(aim_ref[:, sl], (SUBLANES, SCAN_LANES))
        cr, ci = csr_ref[:, sl], csi_ref[:, sl]
        for k in range(N_CHUNKS):
            rows = slice(k * SUBLANES, (k + 1) * SUBLANES)
            vr, vi = sr_ref[rows, sl], si_ref[rows, sl]
            sr_ref[rows, sl] = cr
            si_ref[rows, sl] = ci
            cr, ci = ar * cr - ai * ci + vr, ar * ci + ai * cr + vi
        csr_ref[:, sl] = cr
        csi_ref[:, sl] = ci

    conv = convb_ref[...]
    for k in range(CONV_K):
        conv = conv + convw_ref[k:k + 1, :] * p_ref[k * SUBLANES:k * SUBLANES + MIX_ROWS, :]
    p_ref[0:halo, :] = p_ref[MIX_ROWS:MIX_ROWS + halo, :]

    ys, cbc = [], []
    for q in range(N_BLOCKS):
        sq = jnp.concatenate([sr_ref[:, q * BLOCK_STATE:(q + 1) * BLOCK_STATE],
                              si_ref[:, q * BLOCK_STATE:(q + 1) * BLOCK_STATE]], axis=1)
        yb = jnp.dot(sq.astype(jnp.bfloat16), cw_ref[q], preferred_element_type=f32)
        yb = yb + jnp.dot(ub[q], tw_ref[q], preferred_element_type=f32)
        ys.append(jnp.concatenate(
            [yb[:, jj * LANES:(jj + 1) * LANES].reshape(N_CHUNKS, 1, SUBLANES, LANES)
             for jj in range(SSM_CHUNK)], axis=1).reshape(MIX_ROWS, LANES))
        cols = slice(_O_CB + q * piece, _O_CB + (q + 1) * piece)
        c_b = jnp.dot(xn, win_ref[:, cols], preferred_element_type=f32) + bin_ref[:, cols]
        cbc.append((c_b * conv[:, q * piece:(q + 1) * piece]).astype(jnp.bfloat16))
    cbc = jnp.concatenate(cbc, axis=1)
    y_ssm = jnp.concatenate(ys, axis=1) + dskip_ref[...] * u
    z = jax.nn.gelu(y_ssm, approximate=True).astype(jnp.bfloat16)
    g_ssm = jnp.dot(xn, win_ref[:, _O_GS:_O_GC], preferred_element_type=f32) + bin_ref[:, _O_GS:_O_GC]
    ab = jnp.dot(z, wglu_ref[...], preferred_element_type=f32)
    y_a = ab[:, :D_MODEL]
    y_a = y_a + y_a * jnp.tanh(ab[:, D_MODEL:])
    merged = y_a + y_a * jnp.tanh(g_ssm)
    y_b = jnp.dot(cbc, wco_ref[...], preferred_element_type=f32)
    g_conv = jnp.dot(xn, win_ref[:, _O_GC:IN_COLS], preferred_element_type=f32) + bin_ref[:, _O_GC:IN_COLS]
    merged = merged + (y_b + y_b * jnp.tanh(g_conv))

    mix = jnp.dot(merged.astype(jnp.bfloat16), wout_ref[...], preferred_element_type=f32)
    u_ref[...] = jnp.dot(xn_next, win_ref[:, _O_U:_O_CB], preferred_element_type=f32) + bin_ref[:, _O_U:_O_CB]
    obuf[oslot] = (xbuf[xslot].reshape(MIX_ROWS, D_MODEL) + mix).reshape(SEQ_TILE, BATCH_TILE, D_MODEL)

    for c in out_copies(oslot):
        c.start()

    @pl.when(step == n_steps - 1)
    def _():
        for c in out_copies(oslot) + out_copies(1 - oslot):
            c.wait()


def _resident(shape):
    return pl.BlockSpec(shape, lambda *_: (0,) * len(shape), pipeline_mode=pl.Buffered(1))


def _mixer(x, gmix, win, b_in, a_re, a_im, bw, cw, tw, dskip, wglu, convw, convb, wco, wout):
    batch, seq, d = x.shape
    assert d == D_MODEL and batch % BATCH_TILE == 0 and seq % SEQ_TILE == 0
    weights = (gmix, win, b_in, a_re, a_im, bw, cw, tw, dskip, wglu, convw, convb, wco, wout)
    tile = (SEQ_TILE, BATCH_TILE, D_MODEL)
    assert (batch // BATCH_TILE) * (seq // SEQ_TILE) >= X_SLOTS, "the rings assume at least X_SLOTS grid steps"
    return pl.pallas_call(
        _mixer_kernel,
        out_shape=jax.ShapeDtypeStruct(x.shape, jnp.float32),
        grid=(batch // BATCH_TILE, seq // SEQ_TILE),
        in_specs=[pl.BlockSpec(memory_space=pl.ANY)] + [_resident(w.shape) for w in weights],
        out_specs=pl.BlockSpec(memory_space=pl.ANY),
        scratch_shapes=[
            pltpu.VMEM((X_SLOTS, *tile), jnp.float32),
            pltpu.VMEM((2, *tile), jnp.float32),
            pltpu.SemaphoreType.DMA((X_SLOTS,)),
            pltpu.SemaphoreType.DMA((2,)),
            pltpu.VMEM((2, MIX_ROWS, D_MODEL), jnp.bfloat16),
            pltpu.VMEM((MIX_ROWS, SSM_WIDTH), jnp.float32),
            pltpu.VMEM((CHUNK_ROWS, STATE_LANES), jnp.float32),
            pltpu.VMEM((CHUNK_ROWS, STATE_LANES), jnp.float32),
            pltpu.VMEM((SUBLANES, STATE_LANES), jnp.float32),
            pltpu.VMEM((SUBLANES, STATE_LANES), jnp.float32),
            pltpu.VMEM((MIX_ROWS + (CONV_K - 1) * SUBLANES, D_MODEL), jnp.float32),
        ],
        compiler_params=pltpu.CompilerParams(
            dimension_semantics=("arbitrary", "arbitrary"),
            vmem_limit_bytes=VMEM_LIMIT_BYTES),
        name="mixer",
    )(x, *weights)


def _mlp_kernel(h_ref, gmlp_ref, w1_ref, w2_ref, gfin_ref, o_ref):
    h = h_ref[...]
    hn = _rmsnorm(h, gmlp_ref[...]).astype(jnp.bfloat16)
    acc = h
    for c in range(D_FF // FF_CHUNK):
        a = jnp.maximum(_dot(hn, w1_ref[:, c * FF_CHUNK:(c + 1) * FF_CHUNK]), 0.0)
        acc = acc + _dot((a * a).astype(jnp.bfloat16), w2_ref[c * FF_CHUNK:(c + 1) * FF_CHUNK, :])
    o_ref[...] = _rmsnorm(acc, gfin_ref[...])


def _mlp(h, gmlp, w1, w2, gfin):
    rows, d = h.shape
    assert d == D_MODEL and rows % MLP_ROWS == 0
    tile = pl.BlockSpec((MLP_ROWS, D_MODEL), lambda i: (i, 0))
    weights = (gmlp, w1, w2, gfin)
    in_specs = [tile, _resident(gmlp.shape), _resident(w1.shape), _resident(w2.shape), _resident(gfin.shape)]
    return pl.pallas_call(
        _mlp_kernel,
        out_shape=jax.ShapeDtypeStruct(h.shape, jnp.float32),
        grid=(rows // MLP_ROWS,),
        in_specs=in_specs,
        out_specs=tile,
        compiler_params=pltpu.CompilerParams(
            dimension_semantics=("arbitrary",),
            vmem_limit_bytes=VMEM_LIMIT_BYTES),
        name="mlp",
    )(h, *weights)


def _ssm_weights(lam_re, lam_im, log_dt, b_re, b_im, c_re, c_im):
    n = SSM_CHUNK
    dt = jnp.exp(log_dt)[:, None]

    def a_pow(m):
        m = jnp.asarray(m, jnp.float32)[..., None, None]
        mag = jnp.exp(m * (lam_re * dt))
        return mag * jnp.cos(m * (lam_im * dt)), mag * jnp.sin(m * (lam_im * dt))

    ab_re, ab_im = a_pow(1)
    er, ei = ab_re - 1.0, ab_im
    den = lam_re * lam_re + lam_im * lam_im
    q_re = (er * lam_re + ei * lam_im) / den
    q_im = (ei * lam_re - er * lam_im) / den
    bb_re = q_re[..., None] * b_re - q_im[..., None] * b_im
    bb_im = q_re[..., None] * b_im + q_im[..., None] * b_re

    pr, pi = a_pow(n - 1 - np.arange(n))
    pr, pi = pr[..., None], pi[..., None]
    bw = _group_block_diag(jnp.stack([pr * bb_re - pi * bb_im, pr * bb_im + pi * bb_re]))

    pr, pi = a_pow(np.arange(n + 1))
    pr, pi = pr[..., None], pi[..., None]
    c_re, c_im = c_re.transpose(0, 2, 1), c_im.transpose(0, 2, 1)
    cw_all = _group_block_diag(jnp.stack([c_re * pr - c_im * pi, -(c_re * pi + c_im * pr)]))
    cw_all = cw_all.transpose(0, 2, 1)

    k = jnp.einsum('qrs,qsc->qrc', bw[:, (n - 1) * LANES:, :], cw_all[:, :, :n * LANES],
                   precision=lax.Precision.HIGHEST)
    tw = jnp.concatenate(
        [jnp.concatenate([jnp.zeros((N_BLOCKS, LANES, i * LANES), k.dtype), k[:, :, :(n - i) * LANES]], axis=2)
         for i in range(n)], axis=1)
    an_re, an_im = a_pow(n)
    return an_re, an_im, bw, cw_all[:, :, LANES:], tw


def _group_block_diag(vals):
    parts, steps = vals.shape[:2]
    pair = LANES // SSM_STATE
    x = vals.reshape(parts, steps, N_BLOCKS, GROUPS_PER_BLOCK, SSM_STATE, SSM_GROUP)
    x = x.transpose(2, 1, 3, 5, 0, 4)
    half = np.arange(GROUPS_PER_BLOCK)[:, None] % pair == np.arange(pair)[None, :]
    x = jnp.where(half[None, None, :, None, None, :, None], x[:, :, :, :, :, None, :], 0.0)
    x = x.reshape(N_BLOCKS, steps * LANES, parts, LANES)
    tiled = jnp.concatenate(
        [x[:, :, p, :] for p in range(parts) for _ in range(BLOCK_STATE // LANES)], axis=-1)
    row_pair = (np.arange(steps * LANES) % LANES) // (pair * SSM_GROUP)
    col_tile = (np.arange(parts * BLOCK_STATE) % BLOCK_STATE) // LANES
    return jnp.where(row_pair[:, None] == col_tile[None, :], tiled, 0.0)


def _layer(h, norm_mix_g, w_in, b_in, lam_re, lam_im, log_dt, ssm_b_re, ssm_b_im, ssm_c_re, ssm_c_im,
           ssm_d, w_glu_a, w_glu_b, conv_w, conv_b, w_conv_out, w_out):
    bf16 = jnp.bfloat16
    an_re, an_im, bw, cw, tw = _ssm_weights(lam_re, lam_im, log_dt, ssm_b_re, ssm_b_im, ssm_c_re, ssm_c_im)
    gate_half = np.where(np.arange(IN_COLS) >= _O_GS, 0.5, 1.0).astype(np.float32)[None, :]
    return _mixer(
        h, norm_mix_g.reshape(1, D_MODEL), (w_in * gate_half).astype(bf16), b_in.reshape(1, IN_COLS) * gate_half,
        an_re.reshape(1, STATE_LANES), an_im.reshape(1, STATE_LANES),
        bw.astype(bf16), cw.astype(bf16), tw.astype(bf16),
        ssm_d.reshape(1, SSM_WIDTH), jnp.concatenate([0.25 * w_glu_a, 0.5 * w_glu_b], axis=1).astype(bf16),
        conv_w, conv_b.reshape(1, D_MODEL), (0.5 * w_conv_out).astype(bf16), w_out.astype(bf16))


def kernel(x, norm_mix_g, w_in, b_in, lam_re, lam_im, log_dt, ssm_b_re, ssm_b_im, ssm_c_re, ssm_c_im,
           ssm_d, w_glu_a, w_glu_b, conv_w, conv_b, w_conv_out, w_out, norm_mlp_g, w_ff1, w_ff2,
           norm_final_g):
    depth = norm_mix_g.shape[0]
    assert depth == 1, "the MLP call applies the final norm, so it must be the last layer"
    batch, seq, d = x.shape
    l = 0
    h = _layer(x, norm_mix_g[l], w_in[l], b_in[l], lam_re[l], lam_im[l], log_dt[l],
               ssm_b_re[l], ssm_b_im[l], ssm_c_re[l], ssm_c_im[l], ssm_d[l],
               w_glu_a[l], w_glu_b[l], conv_w[l], conv_b[l], w_conv_out[l], w_out[l])
    out = _mlp(h.reshape(batch * seq, d), norm_mlp_g[l].reshape(1, d), w_ff1[l].astype(jnp.bfloat16),
               w_ff2[l].astype(jnp.bfloat16), norm_final_g.reshape(1, d))
    return out.reshape(batch, seq, d)
```

```python
import numpy as np
import jax
import jax.numpy as jnp
from jax import lax
from jax.experimental import pallas as pl
from jax.experimental.pallas import tpu as pltpu

LANES = 128
SUBLANES = 8
VMEM_LIMIT_BYTES = 56 * 1024 * 1024

D_MODEL = 1024
SSM_GROUP = 16
SSM_WIDTH = D_MODEL // 2
SSM_GROUPS = SSM_WIDTH // SSM_GROUP
SSM_STATE = 64
CONV_K = 3
D_FF = 4 * D_MODEL
NORM_EPS = 1e-6

BATCH_TILE = SUBLANES
SEQ_TILE = 64
MIX_ROWS = BATCH_TILE * SEQ_TILE
X_SLOTS = 3
SSM_CHUNK = 2
N_CHUNKS = SEQ_TILE // SSM_CHUNK
CHUNK_ROWS = N_CHUNKS * SUBLANES
MLP_ROWS = 1024
FF_CHUNK = 1024
MLP_TAIL_BLOCKS = 4

GROUPS_PER_BLOCK = LANES // SSM_GROUP
N_BLOCKS = SSM_GROUPS // GROUPS_PER_BLOCK
BLOCK_STATE = GROUPS_PER_BLOCK * SSM_STATE
STATE_LANES = SSM_GROUPS * SSM_STATE
SCAN_LANES = 512

_O_U = 0
_O_CB = _O_U + SSM_WIDTH
_O_CC = _O_CB + D_MODEL
_O_CV = _O_CC + D_MODEL
_O_GS = _O_CV + D_MODEL
_O_GC = _O_GS + D_MODEL
IN_COLS = _O_GC + D_MODEL


def _rmsnorm(x, g):
    var = jnp.mean(x * x, axis=-1, keepdims=True)
    return x * lax.rsqrt(var + NORM_EPS) * g


def _dot(a, b):
    return jnp.dot(a, b, preferred_element_type=jnp.float32)


def _tile_copies(hbm, buf, sem, step, n_j, slot, to_hbm):
    i, j = lax.div(step, n_j), lax.rem(step, n_j)
    copies = []
    for b in range(BATCH_TILE):
        h = hbm.at[i * BATCH_TILE + b, pl.ds(j * SEQ_TILE, SEQ_TILE), :]
        v = buf.at[slot, :, b, :]
        copies.append(pltpu.make_async_copy(v, h, sem.at[slot]) if to_hbm
                      else pltpu.make_async_copy(h, v, sem.at[slot]))
    return copies


def _mixer_kernel(x_hbm, gmix_ref, win_ref, bin_ref, are_ref, aim_ref, bw_ref, cw_ref, tw_ref, dskip_ref,
                  wglu_ref, convw_ref, convb_ref, wco_ref, wout_ref,
                  o_hbm,
                  xbuf, obuf, in_sem, out_sem, xn16, u_ref, sr_ref, si_ref, csr_ref, csi_ref, p_ref):
    j = pl.program_id(1)
    n_j = pl.num_programs(1)
    n_steps = pl.num_programs(0) * n_j
    step = pl.program_id(0) * n_j + j
    xslot, xnext = lax.rem(step, X_SLOTS), lax.rem(step + 1, X_SLOTS)
    oslot = lax.rem(step, 2)
    halo = (CONV_K - 1) * SUBLANES

    def x_copies(s):
        return _tile_copies(x_hbm, xbuf, in_sem, s, n_j, lax.rem(s, X_SLOTS), False)

    def out_copies(slot):
        return _tile_copies(o_hbm, obuf, out_sem, step, n_j, slot, True)

    @pl.when(step == 0)
    def _():
        for c in x_copies(0) + x_copies(1):
            c.start()
        for c in x_copies(0):
            c.wait()
        xn0 = _rmsnorm(xbuf[0].reshape(MIX_ROWS, D_MODEL), gmix_ref[...]).astype(jnp.bfloat16)
        xn16[0] = xn0
        u_ref[...] = (jnp.dot(xn0, win_ref[:, _O_U:_O_CB], preferred_element_type=jnp.float32)
                      + bin_ref[:, _O_U:_O_CB])

    @pl.when(step + 2 < n_steps)
    def _():
        for c in x_copies(step + 2):
            c.start()

    @pl.when(step + 1 < n_steps)
    def _():
        for c in x_copies(step + 1):
            c.wait()

    @pl.when(step >= 2)
    def _():
        for c in out_copies(oslot):
            c.wait()

    @pl.when(j == 0)
    def _():
        csr_ref[...] = jnp.zeros_like(csr_ref)
        csi_ref[...] = jnp.zeros_like(csi_ref)
        p_ref[0:halo, :] = jnp.zeros((halo, D_MODEL), jnp.float32)

    f32 = jnp.float32
    xn = xn16[oslot]

    u = u_ref[...]
    c_c = jnp.dot(xn, win_ref[:, _O_CC:_O_CV], preferred_element_type=f32) + bin_ref[:, _O_CC:_O_CV]
    u4 = u.reshape(N_CHUNKS, SSM_CHUNK, SUBLANES, SSM_WIDTH)
    piece = D_MODEL // N_BLOCKS
    ub = []
    for q in range(N_BLOCKS):
        ub.append(jnp.concatenate(
            [u4[:, jj, :, q * LANES:(q + 1) * LANES].reshape(CHUNK_ROWS, LANES) for jj in range(SSM_CHUNK)],
            axis=1).astype(jnp.bfloat16))
        v = jnp.dot(ub[q], bw_ref[q], preferred_element_type=f32)
        sr_ref[:, q * BLOCK_STATE:(q + 1) * BLOCK_STATE] = v[:, :BLOCK_STATE]
        si_ref[:, q * BLOCK_STATE:(q + 1) * BLOCK_STATE] = v[:, BLOCK_STATE:]
        cols = slice(_O_CV + q * piece, _O_CV + (q + 1) * piece)
        c_v = jnp.dot(xn, win_ref[:, cols], preferred_element_type=f32) + bin_ref[:, cols]
        p_ref[halo:halo + MIX_ROWS, q * piece:(q + 1) * piece] = c_c[:, q * piece:(q + 1) * piece] * c_v

    xn_next = _rmsnorm(xbuf[xnext].reshape(MIX_ROWS, D_MODEL), gmix_ref[...]).astype(jnp.bfloat16)
    xn16[1 - oslot] = xn_next

    for c in range(STATE_LANES // SCAN_LANES):
        sl = slice(c * SCAN_LANES, (c + 1) * SCAN_LANES)
        ar = jnp.broadcast_to(are_ref[:, sl], (SUBLANES, SCAN_LANES))
        ai = jnp.broadcast_to(aim_ref[:, sl], (SUBLANES, SCAN_LANES))
        cr, ci = csr_ref[:, sl], csi_ref[:, sl]
        for k in range(N_CHUNKS):
            rows = slice(k * SUBLANES, (k + 1) * SUBLANES)
            vr, vi = sr_ref[rows, sl], si_ref[rows, sl]
            sr_ref[rows, sl] = cr
            si_ref[rows, sl] = ci
            cr, ci = ar * cr - ai * ci + vr, ar * ci + ai * cr + vi
        csr_ref[:, sl] = cr
        csi_ref[:, sl] = ci

    conv = convb_ref[...]
    for k in range(CONV_K):
        conv = conv + convw_ref[k:k + 1, :] * p_ref[k * SUBLANES:k * SUBLANES + MIX_ROWS, :]
    p_ref[0:halo, :] = p_ref[MIX_ROWS:MIX_ROWS + halo, :]

    ys, cbc = [], []
    for q in range(N_BLOCKS):
        sq = jnp.concatenate([sr_ref[:, q * BLOCK_STATE:(q + 1) * BLOCK_STATE],
                              si_ref[:, q * BLOCK_STATE:(q + 1) * BLOCK_STATE]], axis=1)
        yb = jnp.dot(sq.astype(jnp.bfloat16), cw_ref[q], preferred_element_type=f32)
        yb = yb + jnp.dot(ub[q], tw_ref[q], preferred_element_type=f32)
        ys.append(jnp.concatenate(
            [yb[:, jj * LANES:(jj + 1) * LANES].reshape(N_CHUNKS, 1, SUBLANES, LANES)
             for jj in range(SSM_CHUNK)], axis=1).reshape(MIX_ROWS, LANES))
        cols = slice(_O_CB + q * piece, _O_CB + (q + 1) * piece)
        c_b = jnp.dot(xn, win_ref[:, cols], preferred_element_type=f32) + bin_ref[:, cols]
        cbc.append((c_b * conv[:, q * piece:(q + 1) * piece]).astype(jnp.bfloat16))
    cbc = jnp.concatenate(cbc, axis=1)
    y_ssm = jnp.concatenate(ys, axis=1) + dskip_ref[...] * u
    z = jax.nn.gelu(y_ssm, approximate=True).astype(jnp.bfloat16)
    g_ssm = jnp.dot(xn, win_ref[:, _O_GS:_O_GC], preferred_element_type=f32) + bin_ref[:, _O_GS:_O_GC]
    ab = jnp.dot(z, wglu_ref[...], preferred_element_type=f32)
    y_a = ab[:, :D_MODEL]
    y_a = y_a + y_a * jnp.tanh(ab[:, D_MODEL:])
    merged = y_a + y_a * jnp.tanh(g_ssm)
    y_b = jnp.dot(cbc, wco_ref[...], preferred_element_type=f32)
    g_conv = jnp.dot(xn, win_ref[:, _O_GC:IN_COLS], preferred_element_type=f32) + bin_ref[:, _O_GC:IN_COLS]
    merged = merged + (y_b + y_b * jnp.tanh(g_conv))

    mix = jnp.dot(merged.astype(jnp.bfloat16), wout_ref[...], preferred_element_type=f32)
    u_ref[...] = jnp.dot(xn_next, win_ref[:, _O_U:_O_CB], preferred_element_type=f32) + bin_ref[:, _O_U:_O_CB]
    obuf[oslot] = (xbuf[xslot].reshape(MIX_ROWS, D_MODEL) + mix).reshape(SEQ_TILE, BATCH_TILE, D_MODEL)

    for c in out_copies(oslot):
        c.start()

    @pl.when(step == n_steps - 1)
    def _():
        for c in out_copies(oslot) + out_copies(1 - oslot):
            c.wait()


def _resident(shape):
    return pl.BlockSpec(shape, lambda *_: (0,) * len(shape), pipeline_mode=pl.Buffered(1))


def _mixer(x, gmix, win, b_in, a_re, a_im, bw, cw, tw, dskip, wglu, convw, convb, wco, wout):
    batch, seq, d = x.shape
    assert d == D_MODEL and batch % BATCH_TILE == 0 and seq % SEQ_TILE == 0
    weights = (gmix, win, b_in, a_re, a_im, bw, cw, tw, dskip, wglu, convw, convb, wco, wout)
    tile = (SEQ_TILE, BATCH_TILE, D_MODEL)
    assert (batch // BATCH_TILE) * (seq // SEQ_TILE) >= X_SLOTS, "the rings assume at least X_SLOTS grid steps"
    return pl.pallas_call(
        _mixer_kernel,
        out_shape=jax.ShapeDtypeStruct(x.shape, jnp.float32),
        grid=(batch // BATCH_TILE, seq // SEQ_TILE),
        in_specs=[pl.BlockSpec(memory_space=pl.ANY)] + [_resident(w.shape) for w in weights],
        out_specs=pl.BlockSpec(memory_space=pl.ANY),
        scratch_shapes=[
            pltpu.VMEM((X_SLOTS, *tile), jnp.float32),
            pltpu.VMEM((2, *tile), jnp.float32),
            pltpu.SemaphoreType.DMA((X_SLOTS,)),
            pltpu.SemaphoreType.DMA((2,)),
            pltpu.VMEM((2, MIX_ROWS, D_MODEL), jnp.bfloat16),
            pltpu.VMEM((MIX_ROWS, SSM_WIDTH), jnp.float32),
            pltpu.VMEM((CHUNK_ROWS, STATE_LANES), jnp.float32),
            pltpu.VMEM((CHUNK_ROWS, STATE_LANES), jnp.float32),
            pltpu.VMEM((SUBLANES, STATE_LANES), jnp.float32),
            pltpu.VMEM((SUBLANES, STATE_LANES), jnp.float32),
            pltpu.VMEM((MIX_ROWS + (CONV_K - 1) * SUBLANES, D_MODEL), jnp.float32),
        ],
        compiler_params=pltpu.CompilerParams(
            dimension_semantics=("arbitrary", "arbitrary"),
            vmem_limit_bytes=VMEM_LIMIT_BYTES),
        name="mixer",
    )(x, *weights)


def _mlp_kernel(h_ref, gmlp_ref, w1_ref, w2_ref, gfin_ref, o_ref):
    h = h_ref[...]
    hn = _rmsnorm(h, gmlp_ref[...]).astype(jnp.bfloat16)
    acc = h
    n_chunks = D_FF // FF_CHUNK
    for c in range(n_chunks - 1):
        a = jnp.maximum(_dot(hn, w1_ref[:, c * FF_CHUNK:(c + 1) * FF_CHUNK]), 0.0)
        acc = acc + _dot((a * a).astype(jnp.bfloat16), w2_ref[c * FF_CHUNK:(c + 1) * FF_CHUNK, :])
    c = n_chunks - 1
    a = jnp.maximum(_dot(hn, w1_ref[:, c * FF_CHUNK:(c + 1) * FF_CHUNK]), 0.0)
    a = (a * a).astype(jnp.bfloat16)
    half = MLP_ROWS // MLP_TAIL_BLOCKS
    for r in range(MLP_TAIL_BLOCKS):
        rows = slice(r * half, (r + 1) * half)
        o_ref[rows, :] = _rmsnorm(acc[rows] + _dot(a[rows], w2_ref[c * FF_CHUNK:(c + 1) * FF_CHUNK, :]),
                                  gfin_ref[...])


def _mlp(h, gmlp, w1, w2, gfin):
    rows, d = h.shape
    assert d == D_MODEL and rows % MLP_ROWS == 0
    tile = pl.BlockSpec((MLP_ROWS, D_MODEL), lambda i: (i, 0))
    weights = (gmlp, w1, w2, gfin)
    in_specs = [tile, _resident(gmlp.shape), _resident(w1.shape), _resident(w2.shape), _resident(gfin.shape)]
    return pl.pallas_call(
        _mlp_kernel,
        out_shape=jax.ShapeDtypeStruct(h.shape, jnp.float32),
        grid=(rows // MLP_ROWS,),
        in_specs=in_specs,
        out_specs=tile,
        compiler_params=pltpu.CompilerParams(
            dimension_semantics=("arbitrary",),
            vmem_limit_bytes=VMEM_LIMIT_BYTES),
        name="mlp",
    )(h, *weights)


def _ssm_weights(lam_re, lam_im, log_dt, b_re, b_im, c_re, c_im):
    n = SSM_CHUNK
    dt = jnp.exp(log_dt)[:, None]

    def a_pow(m):
        m = jnp.asarray(m, jnp.float32)[..., None, None]
        mag = jnp.exp(m * (lam_re * dt))
        return mag * jnp.cos(m * (lam_im * dt)), mag * jnp.sin(m * (lam_im * dt))

    ab_re, ab_im = a_pow(1)
    er, ei = ab_re - 1.0, ab_im
    den = lam_re * lam_re + lam_im * lam_im
    q_re = (er * lam_re + ei * lam_im) / den
    q_im = (ei * lam_re - er * lam_im) / den
    bb_re = q_re[..., None] * b_re - q_im[..., None] * b_im
    bb_im = q_re[..., None] * b_im + q_im[..., None] * b_re

    pr, pi = a_pow(np.concatenate([n - 1 - np.arange(n), np.arange(n + 1)]))
    pr, pi = pr[..., None], pi[..., None]
    shape = (n, SSM_GROUPS, SSM_STATE, SSM_GROUP), (n + 1, SSM_GROUPS, SSM_STATE, SSM_GROUP)
    f_re = jnp.concatenate([jnp.broadcast_to(bb_re, shape[0]),
                            jnp.broadcast_to(c_re.transpose(0, 2, 1), shape[1])])
    f_im = jnp.concatenate([jnp.broadcast_to(bb_im, shape[0]),
                            jnp.broadcast_to(c_im.transpose(0, 2, 1), shape[1])])
    im_sign = np.concatenate([np.ones(n), -np.ones(n + 1)]).astype(np.float32)[:, None, None, None]
    both = _group_block_diag(jnp.stack([pr * f_re - pi * f_im, im_sign * (pr * f_im + pi * f_re)]))
    bw = both[:, :n * LANES, :]
    cw_all = both[:, n * LANES:, :].transpose(0, 2, 1)

    k = jnp.einsum('qrs,qsc->qrc', bw[:, (n - 1) * LANES:, :], cw_all[:, :, :n * LANES],
                   precision=lax.Precision.HIGHEST)
    tw = jnp.concatenate(
        [jnp.concatenate([jnp.zeros((N_BLOCKS, LANES, i * LANES), k.dtype), k[:, :, :(n - i) * LANES]], axis=2)
         for i in range(n)], axis=1)
    an_re, an_im = a_pow(n)
    return an_re, an_im, bw, cw_all[:, :, LANES:], tw


def _group_block_diag(vals):
    parts, steps = vals.shape[:2]
    pair = LANES // SSM_STATE
    x = vals.reshape(parts, steps, N_BLOCKS, GROUPS_PER_BLOCK, SSM_STATE, SSM_GROUP)
    x = x.transpose(2, 1, 3, 5, 0, 4)
    half = np.arange(GROUPS_PER_BLOCK)[:, None] % pair == np.arange(pair)[None, :]
    x = jnp.where(half[None, None, :, None, None, :, None], x[:, :, :, :, :, None, :], 0.0)
    x = x.reshape(N_BLOCKS, steps * LANES, parts, LANES)
    tiled = jnp.concatenate(
        [x[:, :, p, :] for p in range(parts) for _ in range(BLOCK_STATE // LANES)], axis=-1)
    row_pair = (np.arange(steps * LANES) % LANES) // (pair * SSM_GROUP)
    col_tile = (np.arange(parts * BLOCK_STATE) % BLOCK_STATE) // LANES
    return jnp.where(row_pair[:, None] == col_tile[None, :], tiled, 0.0)


def _layer(h, norm_mix_g, w_in, b_in, lam_re, lam_im, log_dt, ssm_b_re, ssm_b_im, ssm_c_re, ssm_c_im,
           ssm_d, w_glu_a, w_glu_b, conv_w, conv_b, w_conv_out, w_out):
    bf16 = jnp.bfloat16
    an_re, an_im, bw, cw, tw = _ssm_weights(lam_re, lam_im, log_dt, ssm_b_re, ssm_b_im, ssm_c_re, ssm_c_im)
    gate_half = np.where(np.arange(IN_COLS) >= _O_GS, 0.5, 1.0).astype(np.float32)[None, :]
    return _mixer(
        h, norm_mix_g.reshape(1, D_MODEL), (w_in * gate_half).astype(bf16), b_in.reshape(1, IN_COLS) * gate_half,
        an_re.reshape(1, STATE_LANES), an_im.reshape(1, STATE_LANES),
        bw.astype(bf16), cw.astype(bf16), tw.astype(bf16),
        ssm_d.reshape(1, SSM_WIDTH), jnp.concatenate([0.25 * w_glu_a, 0.5 * w_glu_b], axis=1).astype(bf16),
        conv_w, conv_b.reshape(1, D_MODEL), (0.5 * w_conv_out).astype(bf16), w_out.astype(bf16))


def kernel(x, norm_mix_g, w_in, b_in, lam_re, lam_im, log_dt, ssm_b_re, ssm_b_im, ssm_c_re, ssm_c_im,
           ssm_d, w_glu_a, w_glu_b, conv_w, conv_b, w_conv_out, w_out, norm_mlp_g, w_ff1, w_ff2,
           norm_final_g):
    depth = norm_mix_g.shape[0]
    assert depth == 1, "the MLP call applies the final norm, so it must be the last layer"
    batch, seq, d = x.shape
    l = 0
    h = _layer(x, norm_mix_g[l], w_in[l], b_in[l], lam_re[l], lam_im[l], log_dt[l],
               ssm_b_re[l], ssm_b_im[l], ssm_c_re[l], ssm_c_im[l], ssm_d[l],
               w_glu_a[l], w_glu_b[l], conv_w[l], conv_b[l], w_conv_out[l], w_out[l])
    out = _mlp(h.reshape(batch * seq, d), norm_mlp_g[l].reshape(1, d), w_ff1[l].astype(jnp.bfloat16),
               w_ff2[l].astype(jnp.bfloat16), norm_final_g.reshape(1, d))
    return out.reshape(batch, seq, d)
```

```python
import numpy as np
import jax
import jax.numpy as jnp
from jax import lax
from jax.experimental import pallas as pl
from jax.experimental.pallas import tpu as pltpu

LANES = 128
SUBLANES = 8
VMEM_LIMIT_BYTES = 56 * 1024 * 1024

D_MODEL = 1024
SSM_GROUP = 16
SSM_WIDTH = D_MODEL // 2
SSM_GROUPS = SSM_WIDTH // SSM_GROUP
SSM_STATE = 64
CONV_K = 3
D_FF = 4 * D_MODEL
NORM_EPS = 1e-6

BATCH_TILE = SUBLANES
SEQ_TILE = 64
MIX_ROWS = BATCH_TILE * SEQ_TILE
X_SLOTS = 3
SSM_CHUNK = 2
N_CHUNKS = SEQ_TILE // SSM_CHUNK
CHUNK_ROWS = N_CHUNKS * SUBLANES
MLP_ROWS = 1024
FF_CHUNK = 1024

GROUPS_PER_BLOCK = LANES // SSM_GROUP
N_BLOCKS = SSM_GROUPS // GROUPS_PER_BLOCK
BLOCK_STATE = GROUPS_PER_BLOCK * SSM_STATE
STATE_LANES = SSM_GROUPS * SSM_STATE
SCAN_LANES = 512

_O_U = 0
_O_CB = _O_U + SSM_WIDTH
_O_CC = _O_CB + D_MODEL
_O_CV = _O_CC + D_MODEL
_O_GS = _O_CV + D_MODEL
_O_GC = _O_GS + D_MODEL
IN_COLS = _O_GC + D_MODEL


def _rmsnorm(x, g):
    var = jnp.mean(x * x, axis=-1, keepdims=True)
    return x * lax.rsqrt(var + NORM_EPS) * g


def _dot(a, b):
    return jnp.dot(a, b, preferred_element_type=jnp.float32)


def _tile_copies(hbm, buf, sem, step, n_j, slot, to_hbm):
    i, j = lax.div(step, n_j), lax.rem(step, n_j)
    copies = []
    for b in range(BATCH_TILE):
        h = hbm.at[i * BATCH_TILE + b, pl.ds(j * SEQ_TILE, SEQ_TILE), :]
        v = buf.at[slot, :, b, :]
        copies.append(pltpu.make_async_copy(v, h, sem.at[slot]) if to_hbm
                      else pltpu.make_async_copy(h, v, sem.at[slot]))
    return copies


def _mixer_kernel(x_hbm, gmix_ref, win_ref, bin_ref, are_ref, aim_ref, bw_ref, cw_ref, tw_ref, dskip_ref,
                  wglu_ref, convw_ref, convb_ref, wco_ref, wout_ref,
                  o_hbm,
                  xbuf, obuf, in_sem, out_sem, xn16, u_ref, sr_ref, si_ref, csr_ref, csi_ref, p_ref):
    j = pl.program_id(1)
    n_j = pl.num_programs(1)
    n_steps = pl.num_programs(0) * n_j
    step = pl.program_id(0) * n_j + j
    xslot, xnext = lax.rem(step, X_SLOTS), lax.rem(step + 1, X_SLOTS)
    oslot = lax.rem(step, 2)
    halo = (CONV_K - 1) * SUBLANES

    def x_copies(s):
        return _tile_copies(x_hbm, xbuf, in_sem, s, n_j, lax.rem(s, X_SLOTS), False)

    def out_copies(slot):
        return _tile_copies(o_hbm, obuf, out_sem, step, n_j, slot, True)

    @pl.when(step == 0)
    def _():
        for c in x_copies(0) + x_copies(1):
            c.start()
        for c in x_copies(0):
            c.wait()
        xn0 = _rmsnorm(xbuf[0].reshape(MIX_ROWS, D_MODEL), gmix_ref[...]).astype(jnp.bfloat16)
        xn16[0] = xn0
        u_ref[...] = (jnp.dot(xn0, win_ref[:, _O_U:_O_CB], preferred_element_type=jnp.float32)
                      + bin_ref[:, _O_U:_O_CB])

    @pl.when(step + 2 < n_steps)
    def _():
        for c in x_copies(step + 2):
            c.start()

    @pl.when(step + 1 < n_steps)
    def _():
        for c in x_copies(step + 1):
            c.wait()

    @pl.when(step >= 2)
    def _():
        for c in out_copies(oslot):
            c.wait()

    @pl.when(j == 0)
    def _():
        csr_ref[...] = jnp.zeros_like(csr_ref)
        csi_ref[...] = jnp.zeros_like(csi_ref)
        p_ref[0:halo, :] = jnp.zeros((halo, D_MODEL), jnp.float32)

    f32 = jnp.float32
    xn = xn16[oslot]

    u = u_ref[...]
    c_c = jnp.dot(xn, win_ref[:, _O_CC:_O_CV], preferred_element_type=f32) + bin_ref[:, _O_CC:_O_CV]
    u4 = u.reshape(N_CHUNKS, SSM_CHUNK, SUBLANES, SSM_WIDTH)
    piece = D_MODEL // N_BLOCKS
    ub = []
    for q in range(N_BLOCKS):
        ub.append(jnp.concatenate(
            [u4[:, jj, :, q * LANES:(q + 1) * LANES].reshape(CHUNK_ROWS, LANES) for jj in range(SSM_CHUNK)],
            axis=1).astype(jnp.bfloat16))
        v = jnp.dot(ub[q], bw_ref[q], preferred_element_type=f32)
        sr_ref[:, q * BLOCK_STATE:(q + 1) * BLOCK_STATE] = v[:, :BLOCK_STATE]
        si_ref[:, q * BLOCK_STATE:(q + 1) * BLOCK_STATE] = v[:, BLOCK_STATE:]
        cols = slice(_O_CV + q * piece, _O_CV + (q + 1) * piece)
        c_v = jnp.dot(xn, win_ref[:, cols], preferred_element_type=f32) + bin_ref[:, cols]
        p_ref[halo:halo + MIX_ROWS, q * piece:(q + 1) * piece] = c_c[:, q * piece:(q + 1) * piece] * c_v

    xn_next = _rmsnorm(xbuf[xnext].reshape(MIX_ROWS, D_MODEL), gmix_ref[...]).astype(jnp.bfloat16)
    xn16[1 - oslot] = xn_next

    for c in range(STATE_LANES // SCAN_LANES):
        sl = slice(c * SCAN_LANES, (c + 1) * SCAN_LANES)
        ar = jnp.broadcast_to(are_ref[:, sl], (SUBLANES, SCAN_LANES))
        ai = jnp.broadcast_to(aim_ref[:, sl], (SUBLANES, SCAN_LANES))
        cr, ci = csr_ref[:, sl], csi_ref[:, sl]
        for k in range(N_CHUNKS):
            rows = slice(k * SUBLANES, (k + 1) * SUBLANES)
            vr, vi = sr_ref[rows, sl], si_ref[rows, sl]
            sr_ref[rows, sl] = cr
            si_ref[rows, sl] = ci
            cr, ci = ar * cr - ai * ci + vr, ar * ci + ai * cr + vi
        csr_ref[:, sl] = cr
        csi_ref[:, sl] = ci

    conv = convb_ref[...]
    for k in range(CONV_K):
        conv = conv + convw_ref[k:k + 1, :] * p_ref[k * SUBLANES:k * SUBLANES + MIX_ROWS, :]
    p_ref[0:halo, :] = p_ref[MIX_ROWS:MIX_ROWS + halo, :]

    ys, cbc = [], []
    for q in range(N_BLOCKS):
        sq = jnp.concatenate([sr_ref[:, q * BLOCK_STATE:(q + 1) * BLOCK_STATE],
                              si_ref[:, q * BLOCK_STATE:(q + 1) * BLOCK_STATE]], axis=1)
        yb = jnp.dot(sq.astype(jnp.bfloat16), cw_ref[q], preferred_element_type=f32)
        yb = yb + jnp.dot(ub[q], tw_ref[q], preferred_element_type=f32)
        ys.append(jnp.concatenate(
            [yb[:, jj * LANES:(jj + 1) * LANES].reshape(N_CHUNKS, 1, SUBLANES, LANES)
             for jj in range(SSM_CHUNK)], axis=1).reshape(MIX_ROWS, LANES))
        cols = slice(_O_CB + q * piece, _O_CB + (q + 1) * piece)
        c_b = jnp.dot(xn, win_ref[:, cols], preferred_element_type=f32) + bin_ref[:, cols]
        cbc.append((c_b * conv[:, q * piece:(q + 1) * piece]).astype(jnp.bfloat16))
    cbc = jnp.concatenate(cbc, axis=1)
    y_ssm = jnp.concatenate(ys, axis=1) + dskip_ref[...] * u
    z = jax.nn.gelu(y_ssm, approximate=True).astype(jnp.bfloat16)
    g_ssm = jnp.dot(xn, win_ref[:, _O_GS:_O_GC], preferred_element_type=f32) + bin_ref[:, _O_GS:_O_GC]
    ab = jnp.dot(z, wglu_ref[...], preferred_element_type=f32)
    y_a = ab[:, :D_MODEL]
    y_a = y_a + y_a * jnp.tanh(ab[:, D_MODEL:])
    merged = y_a + y_a * jnp.tanh(g_ssm)
    y_b = jnp.dot(cbc, wco_ref[...], preferred_element_type=f32)
    g_conv = jnp.dot(xn, win_ref[:, _O_GC:IN_COLS], preferred_element_type=f32) + bin_ref[:, _O_GC:IN_COLS]
    merged = merged + (y_b + y_b * jnp.tanh(g_conv))

    mix = jnp.dot(merged.astype(jnp.bfloat16), wout_ref[...], preferred_element_type=f32)
    u_ref[...] = jnp.dot(xn_next, win_ref[:, _O_U:_O_CB], preferred_element_type=f32) + bin_ref[:, _O_U:_O_CB]
    obuf[oslot] = (xbuf[xslot].reshape(MIX_ROWS, D_MODEL) + mix).reshape(SEQ_TILE, BATCH_TILE, D_MODEL)

    for c in out_copies(oslot):
        c.start()

    @pl.when(step == n_steps - 1)
    def _():
        for c in out_copies(oslot) + out_copies(1 - oslot):
            c.wait()


def _resident(shape):
    return pl.BlockSpec(shape, lambda *_: (0,) * len(shape), pipeline_mode=pl.Buffered(1))


def _mixer(x, gmix, win, b_in, a_re, a_im, bw, cw, tw, dskip, wglu, convw, convb, wco, wout):
    batch, seq, d = x.shape
    assert d == D_MODEL and batch % BATCH_TILE == 0 and seq % SEQ_TILE == 0
    weights = (gmix, win, b_in, a_re, a_im, bw, cw, tw, dskip, wglu, convw, convb, wco, wout)
    tile = (SEQ_TILE, BATCH_TILE, D_MODEL)
    assert (batch // BATCH_TILE) * (seq // SEQ_TILE) >= X_SLOTS, "the rings assume at least X_SLOTS grid steps"
    return pl.pallas_call(
        _mixer_kernel,
        out_shape=jax.ShapeDtypeStruct(x.shape, jnp.float32),
        grid=(batch // BATCH_TILE, seq // SEQ_TILE),
        in_specs=[pl.BlockSpec(memory_space=pl.ANY)] + [_resident(w.shape) for w in weights],
        out_specs=pl.BlockSpec(memory_space=pl.ANY),
        scratch_shapes=[
            pltpu.VMEM((X_SLOTS, *tile), jnp.float32),
            pltpu.VMEM((2, *tile), jnp.float32),
            pltpu.SemaphoreType.DMA((X_SLOTS,)),
            pltpu.SemaphoreType.DMA((2,)),
            pltpu.VMEM((2, MIX_ROWS, D_MODEL), jnp.bfloat16),
            pltpu.VMEM((MIX_ROWS, SSM_WIDTH), jnp.float32),
            pltpu.VMEM((CHUNK_ROWS, STATE_LANES), jnp.float32),
            pltpu.VMEM((CHUNK_ROWS, STATE_LANES), jnp.float32),
            pltpu.VMEM((SUBLANES, STATE_LANES), jnp.float32),
            pltpu.VMEM((SUBLANES, STATE_LANES), jnp.float32),
            pltpu.VMEM((MIX_ROWS + (CONV_K - 1) * SUBLANES, D_MODEL), jnp.float32),
        ],
        compiler_params=pltpu.CompilerParams(
            dimension_semantics=("arbitrary", "arbitrary"),
            vmem_limit_bytes=VMEM_LIMIT_BYTES),
        name="mixer",
    )(x, *weights)


def _mlp_kernel(h_ref, gmlp_ref, w1_ref, w2_ref, gfin_ref, o_ref):
    h = h_ref[...]
    hn = _rmsnorm(h, gmlp_ref[...]).astype(jnp.bfloat16)
    acc = h
    for c in range(D_FF // FF_CHUNK):
        a = jnp.maximum(_dot(hn, w1_ref[:, c * FF_CHUNK:(c + 1) * FF_CHUNK]), 0.0)
        acc = acc + _dot((a * a).astype(jnp.bfloat16), w2_ref[c * FF_CHUNK:(c + 1) * FF_CHUNK, :])
    o_ref[...] = _rmsnorm(acc, gfin_ref[...])


def _mlp(h, gmlp, w1, w2, gfin):
    rows, d = h.shape
    assert d == D_MODEL and rows % MLP_ROWS == 0
    tile = pl.BlockSpec((MLP_ROWS, D_MODEL), lambda i: (i, 0))
    weights = (gmlp, w1, w2, gfin)
    in_specs = [tile, _resident(gmlp.shape), _resident(w1.shape), _resident(w2.shape), _resident(gfin.shape)]
    return pl.pallas_call(
        _mlp_kernel,
        out_shape=jax.ShapeDtypeStruct(h.shape, jnp.float32),
        grid=(rows // MLP_ROWS,),
        in_specs=in_specs,
        out_specs=tile,
        compiler_params=pltpu.CompilerParams(
            dimension_semantics=("arbitrary",),
            vmem_limit_bytes=VMEM_LIMIT_BYTES),
        name="mlp",
    )(h, *weights)


def _ssm_weights(lam_re, lam_im, log_dt, b_re, b_im, c_re, c_im):
    n = SSM_CHUNK
    dt = jnp.exp(log_dt)[:, None]

    def a_pow(m):
        m = jnp.asarray(m, jnp.float32)[..., None, None]
        mag = jnp.exp(m * (lam_re * dt))
        return mag * jnp.cos(m * (lam_im * dt)), mag * jnp.sin(m * (lam_im * dt))

    ab_re, ab_im = a_pow(1)
    er, ei = ab_re - 1.0, ab_im
    den = lam_re * lam_re + lam_im * lam_im
    q_re = (er * lam_re + ei * lam_im) / den
    q_im = (ei * lam_re - er * lam_im) / den
    bb_re = q_re[..., None] * b_re - q_im[..., None] * b_im
    bb_im = q_re[..., None] * b_im + q_im[..., None] * b_re

    pr, pi = a_pow(n - 1 - np.arange(n))
    pr, pi = pr[..., None], pi[..., None]
    bw = _group_block_diag(jnp.stack([pr * bb_re - pi * bb_im, pr * bb_im + pi * bb_re]))

    pr, pi = a_pow(np.arange(n + 1))
    pr, pi = pr[..., None], pi[..., None]
    c_re, c_im = c_re.transpose(0, 2, 1), c_im.transpose(0, 2, 1)
    cw_all = _group_block_diag(jnp.stack([c_re * pr - c_im * pi, -(c_re * pi + c_im * pr)]))
    cw_all = cw_all.transpose(0, 2, 1)

    k = jnp.einsum('qrs,qsc->qrc', bw[:, (n - 1) * LANES:, :], cw_all[:, :, :n * LANES],
                   precision=lax.Precision.HIGHEST)
    tw = jnp.concatenate(
        [jnp.concatenate([jnp.zeros((N_BLOCKS, LANES, i * LANES), k.dtype), k[:, :, :(n - i) * LANES]], axis=2)
         for i in range(n)], axis=1)
    an_re, an_im = a_pow(n)
    return an_re, an_im, bw, cw_all[:, :, LANES:], tw


def _group_block_diag(vals):
    parts, steps = vals.shape[:2]
    pair = LANES // SSM_STATE
    x = vals.reshape(parts, steps, N_BLOCKS, GROUPS_PER_BLOCK, SSM_STATE, SSM_GROUP)
    x = x.transpose(2, 1, 3, 5, 0, 4)
    half = np.arange(GROUPS_PER_BLOCK)[:, None] % pair == np.arange(pair)[None, :]
    x = jnp.where(half[None, None, :, None, None, :, None], x[:, :, :, :, :, None, :], 0.0)
    x = x.reshape(N_BLOCKS, steps * LANES, parts, LANES)
    tiled = jnp.concatenate(
        [x[:, :, p, :] for p in range(parts) for _ in range(BLOCK_STATE // LANES)], axis=-1)
    row_pair = (np.arange(steps * LANES) % LANES) // (pair * SSM_GROUP)
    col_tile = (np.arange(parts * BLOCK_STATE) % BLOCK_STATE) // LANES
    return jnp.where(row_pair[:, None] == col_tile[None, :], tiled, 0.0)


def _layer(h, norm_mix_g, w_in, b_in, lam_re, lam_im, log_dt, ssm_b_re, ssm_b_im, ssm_c_re, ssm_c_im,
           ssm_d, w_glu_a, w_glu_b, conv_w, conv_b, w_conv_out, w_out):
    bf16 = jnp.bfloat16
    an_re, an_im, bw, cw, tw = _ssm_weights(lam_re, lam_im, log_dt, ssm_b_re, ssm_b_im, ssm_c_re, ssm_c_im)
    gate_half = np.where(np.arange(IN_COLS) >= _O_GS, 0.5, 1.0).astype(np.float32)[None, :]
    return _mixer(
        h, norm_mix_g.reshape(1, D_MODEL), (w_in * gate_half).astype(bf16), b_in.reshape(1, IN_COLS) * gate_half,
        an_re.reshape(1, STATE_LANES), an_im.reshape(1, STATE_LANES),
        bw.astype(bf16), cw.astype(bf16), tw.astype(bf16),
        ssm_d.reshape(1, SSM_WIDTH), jnp.concatenate([0.25 * w_glu_a, 0.5 * w_glu_b], axis=1).astype(bf16),
        conv_w, conv_b.reshape(1, D_MODEL), (0.5 * w_conv_out).astype(bf16), w_out.astype(bf16))


def kernel(x, norm_mix_g, w_in, b_in, lam_re, lam_im, log_dt, ssm_b_re, ssm_b_im, ssm_c_re, ssm_c_im,
           ssm_d, w_glu_a, w_glu_b, conv_w, conv_b, w_conv_out, w_out, norm_mlp_g, w_ff1, w_ff2,
           norm_final_g):
    depth = norm_mix_g.shape[0]
    assert depth == 1, "the MLP call applies the final norm, so it must be the last layer"
    batch, seq, d = x.shape
    l = 0
    h = _layer(x, norm_mix_g[l], w_in[l], b_in[l], lam_re[l], lam_im[l], log_dt[l],
               ssm_b_re[l], ssm_b_im[l], ssm_c_re[l], ssm_c_im[l], ssm_d[l],
               w_glu_a[l], w_glu_b[l], conv_w[l], conv_b[l], w_conv_out[l], w_out[l])
    out = _mlp(h.reshape(batch * seq, d), norm_mlp_g[l].reshape(1, d), w_ff1[l].astype(jnp.bfloat16),
               w_ff2[l].astype(jnp.bfloat16), norm_final_g.reshape(1, d))
    return out.reshape(batch, seq, d)
```

```python
import numpy as np
import jax
import jax.numpy as jnp
from jax import lax
from jax.experimental import pallas as pl
from jax.experimental.pallas import tpu as pltpu

LANES = 128
SUBLANES = 8
VMEM_LIMIT_BYTES = 56 * 1024 * 1024

D_MODEL = 1024
SSM_GROUP = 16
SSM_WIDTH = D_MODEL // 2
SSM_GROUPS = SSM_WIDTH // SSM_GROUP
SSM_STATE = 64
CONV_K = 3
D_FF = 4 * D_MODEL
NORM_EPS = 1e-6

BATCH_TILE = SUBLANES
SEQ_TILE = 32
MIX_ROWS = BATCH_TILE * SEQ_TILE
X_SLOTS = 3
SSM_CHUNK = 2
N_CHUNKS = SEQ_TILE // SSM_CHUNK
CHUNK_ROWS = N_CHUNKS * SUBLANES
FF_CHUNK = 1024

GROUPS_PER_BLOCK = LANES // SSM_GROUP
N_BLOCKS = SSM_GROUPS // GROUPS_PER_BLOCK
BLOCK_STATE = GROUPS_PER_BLOCK * SSM_STATE
STATE_LANES = SSM_GROUPS * SSM_STATE
SCAN_LANES = 512

_O_U = 0
_O_CB = _O_U + SSM_WIDTH
_O_CC = _O_CB + D_MODEL
_O_CV = _O_CC + D_MODEL
_O_GS = _O_CV + D_MODEL
_O_GC = _O_GS + D_MODEL
IN_COLS = _O_GC + D_MODEL


def _rmsnorm(x, g):
    var = jnp.mean(x * x, axis=-1, keepdims=True)
    return x * lax.rsqrt(var + NORM_EPS) * g


def _dot(a, b):
    return jnp.dot(a, b, preferred_element_type=jnp.float32)


def _tile_copies(hbm, buf, sem, step, n_j, slot, to_hbm):
    i, j = lax.div(step, n_j), lax.rem(step, n_j)
    copies = []
    for b in range(BATCH_TILE):
        h = hbm.at[i * BATCH_TILE + b, pl.ds(j * SEQ_TILE, SEQ_TILE), :]
        v = buf.at[slot, :, b, :]
        copies.append(pltpu.make_async_copy(v, h, sem.at[slot]) if to_hbm
                      else pltpu.make_async_copy(h, v, sem.at[slot]))
    return copies


def _mixer_kernel(x_hbm, gmix_ref, win_ref, bin_ref, are_ref, aim_ref, bw_ref, cw_ref, tw_ref, dskip_ref,
                  wglu_ref, convw_ref, convb_ref, wco_ref, wout_ref, gmlp_ref, w1_ref, w2_ref, gfin_ref,
                  o_hbm,
                  xbuf, obuf, in_sem, out_sem, xn16, u_ref, cc_ref, sr_ref, si_ref, csr_ref, csi_ref, p_ref):
    j = pl.program_id(1)
    n_j = pl.num_programs(1)
    n_steps = pl.num_programs(0) * n_j
    step = pl.program_id(0) * n_j + j
    xslot, xnext = lax.rem(step, X_SLOTS), lax.rem(step + 1, X_SLOTS)
    oslot = lax.rem(step, 2)
    halo = (CONV_K - 1) * SUBLANES

    def x_copies(s):
        return _tile_copies(x_hbm, xbuf, in_sem, s, n_j, lax.rem(s, X_SLOTS), False)

    def out_copies(slot):
        return _tile_copies(o_hbm, obuf, out_sem, step, n_j, slot, True)

    @pl.when(step == 0)
    def _():
        for c in x_copies(0) + x_copies(1):
            c.start()
        for c in x_copies(0):
            c.wait()
        xn0 = _rmsnorm(xbuf[0].reshape(MIX_ROWS, D_MODEL), gmix_ref[...]).astype(jnp.bfloat16)
        xn16[0] = xn0
        u_ref[...] = (jnp.dot(xn0, win_ref[:, _O_U:_O_CB], preferred_element_type=jnp.float32)
                      + bin_ref[:, _O_U:_O_CB])
        cc_ref[...] = (jnp.dot(xn0, win_ref[:, _O_CC:_O_CV], preferred_element_type=jnp.float32)
                       + bin_ref[:, _O_CC:_O_CV])

    @pl.when(step + 2 < n_steps)
    def _():
        for c in x_copies(step + 2):
            c.start()

    @pl.when(step + 1 < n_steps)
    def _():
        for c in x_copies(step + 1):
            c.wait()

    @pl.when(step >= 2)
    def _():
        for c in out_copies(oslot):
            c.wait()

    @pl.when(j == 0)
    def _():
        csr_ref[...] = jnp.zeros_like(csr_ref)
        csi_ref[...] = jnp.zeros_like(csi_ref)
        p_ref[0:halo, :] = jnp.zeros((halo, D_MODEL), jnp.float32)

    f32 = jnp.float32
    xn = xn16[oslot]

    u = u_ref[...]
    c_c = cc_ref[...]
    u4 = u.reshape(N_CHUNKS, SSM_CHUNK, SUBLANES, SSM_WIDTH)
    piece = D_MODEL // N_BLOCKS
    ub = []
    for q in range(N_BLOCKS):
        ub.append(jnp.concatenate(
            [u4[:, jj, :, q * LANES:(q + 1) * LANES].reshape(CHUNK_ROWS, LANES) for jj in range(SSM_CHUNK)],
            axis=1).astype(jnp.bfloat16))
        v = jnp.dot(ub[q], bw_ref[q], preferred_element_type=f32)
        sr_ref[:, q * BLOCK_STATE:(q + 1) * BLOCK_STATE] = v[:, :BLOCK_STATE]
        si_ref[:, q * BLOCK_STATE:(q + 1) * BLOCK_STATE] = v[:, BLOCK_STATE:]
        cols = slice(_O_CV + q * piece, _O_CV + (q + 1) * piece)
        c_v = jnp.dot(xn, win_ref[:, cols], preferred_element_type=f32) + bin_ref[:, cols]
        p_ref[halo:halo + MIX_ROWS, q * piece:(q + 1) * piece] = c_c[:, q * piece:(q + 1) * piece] * c_v

    xn_next = _rmsnorm(xbuf[xnext].reshape(MIX_ROWS, D_MODEL), gmix_ref[...]).astype(jnp.bfloat16)
    xn16[1 - oslot] = xn_next

    for c in range(STATE_LANES // SCAN_LANES):
        sl = slice(c * SCAN_LANES, (c + 1) * SCAN_LANES)
        ar = jnp.broadcast_to(are_ref[:, sl], (SUBLANES, SCAN_LANES))
        ai = jnp.broadcast_to(aim_ref[:, sl], (SUBLANES, SCAN_LANES))
        cr, ci = csr_ref[:, sl], csi_ref[:, sl]
        for k in range(N_CHUNKS):
            rows = slice(k * SUBLANES, (k + 1) * SUBLANES)
            vr, vi = sr_ref[rows, sl], si_ref[rows, sl]
            sr_ref[rows, sl] = cr
            si_ref[rows, sl] = ci
            cr, ci = ar * cr - ai * ci + vr, ar * ci + ai * cr + vi
        csr_ref[:, sl] = cr
        csi_ref[:, sl] = ci

    conv = convb_ref[...]
    for k in range(CONV_K):
        conv = conv + convw_ref[k:k + 1, :] * p_ref[k * SUBLANES:k * SUBLANES + MIX_ROWS, :]
    p_ref[0:halo, :] = p_ref[MIX_ROWS:MIX_ROWS + halo, :]

    ys, cbc = [], []
    for q in range(N_BLOCKS):
        sq = jnp.concatenate([sr_ref[:, q * BLOCK_STATE:(q + 1) * BLOCK_STATE],
                              si_ref[:, q * BLOCK_STATE:(q + 1) * BLOCK_STATE]], axis=1)
        yb = jnp.dot(sq.astype(jnp.bfloat16), cw_ref[q], preferred_element_type=f32)
        yb = yb + jnp.dot(ub[q], tw_ref[q], preferred_element_type=f32)
        ys.append(jnp.concatenate(
            [yb[:, jj * LANES:(jj + 1) * LANES].reshape(N_CHUNKS, 1, SUBLANES, LANES)
             for jj in range(SSM_CHUNK)], axis=1).reshape(MIX_ROWS, LANES))
        cols = slice(_O_CB + q * piece, _O_CB + (q + 1) * piece)
        c_b = jnp.dot(xn, win_ref[:, cols], preferred_element_type=f32) + bin_ref[:, cols]
        cbc.append((c_b * conv[:, q * piece:(q + 1) * piece]).astype(jnp.bfloat16))
    cbc = jnp.concatenate(cbc, axis=1)
    y_ssm = jnp.concatenate(ys, axis=1) + dskip_ref[...] * u
    z = jax.nn.gelu(y_ssm, approximate=True).astype(jnp.bfloat16)
    g_ssm = jnp.dot(xn, win_ref[:, _O_GS:_O_GC], preferred_element_type=f32) + bin_ref[:, _O_GS:_O_GC]
    ab = jnp.dot(z, wglu_ref[...], preferred_element_type=f32)
    y_a = ab[:, :D_MODEL]
    y_a = y_a + y_a * jnp.tanh(ab[:, D_MODEL:])
    merged = y_a + y_a * jnp.tanh(g_ssm)
    y_b = jnp.dot(cbc, wco_ref[...], preferred_element_type=f32)
    g_conv = jnp.dot(xn, win_ref[:, _O_GC:IN_COLS], preferred_element_type=f32) + bin_ref[:, _O_GC:IN_COLS]
    merged = merged + (y_b + y_b * jnp.tanh(g_conv))

    mix = jnp.dot(merged.astype(jnp.bfloat16), wout_ref[...], preferred_element_type=f32)
    cc_ref[...] = jnp.dot(xn_next, win_ref[:, _O_CC:_O_CV], preferred_element_type=f32) + bin_ref[:, _O_CC:_O_CV]
    h = xbuf[xslot].reshape(MIX_ROWS, D_MODEL) + mix

    hn = _rmsnorm(h, gmlp_ref[...]).astype(jnp.bfloat16)
    acc = h
    for c in range(D_FF // FF_CHUNK):
        a = jnp.maximum(_dot(hn, w1_ref[:, c * FF_CHUNK:(c + 1) * FF_CHUNK]), 0.0)
        acc = acc + _dot((a * a).astype(jnp.bfloat16), w2_ref[c * FF_CHUNK:(c + 1) * FF_CHUNK, :])
    u_ref[...] = jnp.dot(xn_next, win_ref[:, _O_U:_O_CB], preferred_element_type=f32) + bin_ref[:, _O_U:_O_CB]
    obuf[oslot] = _rmsnorm(acc, gfin_ref[...]).reshape(SEQ_TILE, BATCH_TILE, D_MODEL)

    for c in out_copies(oslot):
        c.start()

    @pl.when(step == n_steps - 1)
    def _():
        for c in out_copies(oslot) + out_copies(1 - oslot):
            c.wait()


def _resident(shape):
    return pl.BlockSpec(shape, lambda *_: (0,) * len(shape), pipeline_mode=pl.Buffered(1))


def _mixer(x, gmix, win, b_in, a_re, a_im, bw, cw, tw, dskip, wglu, convw, convb, wco, wout, gmlp, w1, w2, gfin):
    batch, seq, d = x.shape
    assert d == D_MODEL and batch % BATCH_TILE == 0 and seq % SEQ_TILE == 0
    weights = (gmix, win, b_in, a_re, a_im, bw, cw, tw, dskip, wglu, convw, convb, wco, wout, gmlp, w1, w2, gfin)
    tile = (SEQ_TILE, BATCH_TILE, D_MODEL)
    assert (batch // BATCH_TILE) * (seq // SEQ_TILE) >= X_SLOTS, "the rings assume at least X_SLOTS grid steps"
    return pl.pallas_call(
        _mixer_kernel,
        out_shape=jax.ShapeDtypeStruct(x.shape, jnp.float32),
        grid=(batch // BATCH_TILE, seq // SEQ_TILE),
        in_specs=[pl.BlockSpec(memory_space=pl.ANY)] + [_resident(w.shape) for w in weights],
        out_specs=pl.BlockSpec(memory_space=pl.ANY),
        scratch_shapes=[
            pltpu.VMEM((X_SLOTS, *tile), jnp.float32),
            pltpu.VMEM((2, *tile), jnp.float32),
            pltpu.SemaphoreType.DMA((X_SLOTS,)),
            pltpu.SemaphoreType.DMA((2,)),
            pltpu.VMEM((2, MIX_ROWS, D_MODEL), jnp.bfloat16),
            pltpu.VMEM((MIX_ROWS, SSM_WIDTH), jnp.float32),
            pltpu.VMEM((MIX_ROWS, D_MODEL), jnp.float32),
            pltpu.VMEM((CHUNK_ROWS, STATE_LANES), jnp.float32),
            pltpu.VMEM((CHUNK_ROWS, STATE_LANES), jnp.float32),
            pltpu.VMEM((SUBLANES, STATE_LANES), jnp.float32),
            pltpu.VMEM((SUBLANES, STATE_LANES), jnp.float32),
            pltpu.VMEM((MIX_ROWS + (CONV_K - 1) * SUBLANES, D_MODEL), jnp.float32),
        ],
        compiler_params=pltpu.CompilerParams(
            dimension_semantics=("arbitrary", "arbitrary"),
            vmem_limit_bytes=VMEM_LIMIT_BYTES),
        name="block",
    )(x, *weights)


def _ssm_weights(lam_re, lam_im, log_dt, b_re, b_im, c_re, c_im):
    n = SSM_CHUNK
    dt = jnp.exp(log_dt)[:, None]

    def a_pow(m):
        m = jnp.asarray(m, jnp.float32)[..., None, None]
        mag = jnp.exp(m * (lam_re * dt))
        return mag * jnp.cos(m * (lam_im * dt)), mag * jnp.sin(m * (lam_im * dt))

    ab_re, ab_im = a_pow(1)
    er, ei = ab_re - 1.0, ab_im
    den = lam_re * lam_re + lam_im * lam_im
    q_re = (er * lam_re + ei * lam_im) / den
    q_im = (ei * lam_re - er * lam_im) / den
    bb_re = q_re[..., None] * b_re - q_im[..., None] * b_im
    bb_im = q_re[..., None] * b_im + q_im[..., None] * b_re

    pr, pi = a_pow(n - 1 - np.arange(n))
    pr, pi = pr[..., None], pi[..., None]
    bw = _group_block_diag(jnp.stack([pr * bb_re - pi * bb_im, pr * bb_im + pi * bb_re]))

    pr, pi = a_pow(np.arange(n + 1))
    pr, pi = pr[..., None], pi[..., None]
    c_re, c_im = c_re.transpose(0, 2, 1), c_im.transpose(0, 2, 1)
    cw_all = _group_block_diag(jnp.stack([c_re * pr - c_im * pi, -(c_re * pi + c_im * pr)]))
    cw_all = cw_all.transpose(0, 2, 1)

    k = jnp.einsum('qrs,qsc->qrc', bw[:, (n - 1) * LANES:, :], cw_all[:, :, :n * LANES],
                   precision=lax.Precision.HIGHEST)
    tw = jnp.concatenate(
        [jnp.concatenate([jnp.zeros((N_BLOCKS, LANES, i * LANES), k.dtype), k[:, :, :(n - i) * LANES]], axis=2)
         for i in range(n)], axis=1)
    an_re, an_im = a_pow(n)
    return an_re, an_im, bw, cw_all[:, :, LANES:], tw


def _group_block_diag(vals):
    parts, steps = vals.shape[:2]
    pair = LANES // SSM_STATE
    x = vals.reshape(parts, steps, N_BLOCKS, GROUPS_PER_BLOCK, SSM_STATE, SSM_GROUP)
    x = x.transpose(2, 1, 3, 5, 0, 4)
    half = np.arange(GROUPS_PER_BLOCK)[:, None] % pair == np.arange(pair)[None, :]
    x = jnp.where(half[None, None, :, None, None, :, None], x[:, :, :, :, :, None, :], 0.0)
    x = x.reshape(N_BLOCKS, steps * LANES, parts, LANES)
    tiled = jnp.concatenate(
        [x[:, :, p, :] for p in range(parts) for _ in range(BLOCK_STATE // LANES)], axis=-1)
    row_pair = (np.arange(steps * LANES) % LANES) // (pair * SSM_GROUP)
    col_tile = (np.arange(parts * BLOCK_STATE) % BLOCK_STATE) // LANES
    return jnp.where(row_pair[:, None] == col_tile[None, :], tiled, 0.0)


def _layer(h, norm_mix_g, w_in, b_in, lam_re, lam_im, log_dt, ssm_b_re, ssm_b_im, ssm_c_re, ssm_c_im,
           ssm_d, w_glu_a, w_glu_b, conv_w, conv_b, w_conv_out, w_out, norm_mlp_g, w_ff1, w_ff2, norm_final_g):
    bf16 = jnp.bfloat16
    an_re, an_im, bw, cw, tw = _ssm_weights(lam_re, lam_im, log_dt, ssm_b_re, ssm_b_im, ssm_c_re, ssm_c_im)
    gate_half = np.where(np.arange(IN_COLS) >= _O_GS, 0.5, 1.0).astype(np.float32)[None, :]
    return _mixer(
        h, norm_mix_g.reshape(1, D_MODEL), (w_in * gate_half).astype(bf16), b_in.reshape(1, IN_COLS) * gate_half,
        an_re.reshape(1, STATE_LANES), an_im.reshape(1, STATE_LANES),
        bw.astype(bf16), cw.astype(bf16), tw.astype(bf16),
        ssm_d.reshape(1, SSM_WIDTH), jnp.concatenate([0.25 * w_glu_a, 0.5 * w_glu_b], axis=1).astype(bf16),
        conv_w, conv_b.reshape(1, D_MODEL), (0.5 * w_conv_out).astype(bf16), w_out.astype(bf16),
        norm_mlp_g.reshape(1, D_MODEL), w_ff1.astype(bf16), w_ff2.astype(bf16), norm_final_g.reshape(1, D_MODEL))


def kernel(x, norm_mix_g, w_in, b_in, lam_re, lam_im, log_dt, ssm_b_re, ssm_b_im, ssm_c_re, ssm_c_im,
           ssm_d, w_glu_a, w_glu_b, conv_w, conv_b, w_conv_out, w_out, norm_mlp_g, w_ff1, w_ff2,
           norm_final_g):
    depth = norm_mix_g.shape[0]
    assert depth == 1, "the fused call applies the final norm, so it must be the last layer"
    l = 0
    return _layer(x, norm_mix_g[l], w_in[l], b_in[l], lam_re[l], lam_im[l], log_dt[l],
                  ssm_b_re[l], ssm_b_im[l], ssm_c_re[l], ssm_c_im[l], ssm_d[l],
                  w_glu_a[l], w_glu_b[l], conv_w[l], conv_b[l], w_conv_out[l], w_out[l],
                  norm_mlp_g[l], w_ff1[l], w_ff2[l], norm_final_g)
```

```python
import numpy as np
import jax
import jax.numpy as jnp
from jax import lax
from jax.experimental import pallas as pl
from jax.experimental.pallas import tpu as pltpu

LANES = 128
SUBLANES = 8
VMEM_LIMIT_BYTES = 56 * 1024 * 1024

D_MODEL = 1024
SSM_GROUP = 16
SSM_WIDTH = D_MODEL // 2
SSM_GROUPS = SSM_WIDTH // SSM_GROUP
SSM_STATE = 64
CONV_K = 3
D_FF = 4 * D_MODEL
NORM_EPS = 1e-6

BATCH_TILE = SUBLANES
SEQ_TILE = 64
MIX_ROWS = BATCH_TILE * SEQ_TILE
X_SLOTS = 3
SSM_CHUNK = 2
N_CHUNKS = SEQ_TILE // SSM_CHUNK
CHUNK_ROWS = N_CHUNKS * SUBLANES
MLP_ROWS = 1024
FF_CHUNK = 1024
MLP_TAIL_BLOCKS = 4

GROUPS_PER_BLOCK = LANES // SSM_GROUP
N_BLOCKS = SSM_GROUPS // GROUPS_PER_BLOCK
BLOCK_STATE = GROUPS_PER_BLOCK * SSM_STATE
STATE_LANES = SSM_GROUPS * SSM_STATE
SCAN_LANES = 512

_O_U = 0
_O_CB = _O_U + SSM_WIDTH
_O_CC = _O_CB + D_MODEL
_O_CV = _O_CC + D_MODEL
_O_GS = _O_CV + D_MODEL
_O_GC = _O_GS + D_MODEL
IN_COLS = _O_GC + D_MODEL


def _rmsnorm(x, g):
    var = jnp.mean(x * x, axis=-1, keepdims=True)
    return x * lax.rsqrt(var + NORM_EPS) * g


def _dot(a, b):
    return jnp.dot(a, b, preferred_element_type=jnp.float32)


def _tile_copies(hbm, buf, sem, step, n_j, slot, to_hbm):
    i, j = lax.div(step, n_j), lax.rem(step, n_j)
    copies = []
    for b in range(BATCH_TILE):
        h = hbm.at[i * BATCH_TILE + b, pl.ds(j * SEQ_TILE, SEQ_TILE), :]
        v = buf.at[slot, :, b, :]
        copies.append(pltpu.make_async_copy(v, h, sem.at[slot]) if to_hbm
                      else pltpu.make_async_copy(h, v, sem.at[slot]))
    return copies


def _mixer_kernel(x_hbm, gmix_ref, win_ref, bin_ref, are_ref, aim_ref, bw_ref, cw_ref, tw_ref, dskip_ref,
                  wglu_ref, convw_ref, convb_ref, wco_ref, wout_ref,
                  o_hbm,
                  xbuf, obuf, in_sem, out_sem, xn16, u_ref, sr_ref, si_ref, csr_ref, csi_ref, p_ref):
    j = pl.program_id(1)
    n_j = pl.num_programs(1)
    n_steps = pl.num_programs(0) * n_j
    step = pl.program_id(0) * n_j + j
    xslot, xnext = lax.rem(step, X_SLOTS), lax.rem(step + 1, X_SLOTS)
    oslot = lax.rem(step, 2)
    halo = (CONV_K - 1) * SUBLANES

    def x_copies(s):
        return _tile_copies(x_hbm, xbuf, in_sem, s, n_j, lax.rem(s, X_SLOTS), False)

    def out_copies(slot):
        return _tile_copies(o_hbm, obuf, out_sem, step, n_j, slot, True)

    @pl.when(step == 0)
    def _():
        for c in x_copies(0) + x_copies(1):
            c.start()
        for c in x_copies(0):
            c.wait()
        xn0 = _rmsnorm(xbuf[0].reshape(MIX_ROWS, D_MODEL), gmix_ref[...]).astype(jnp.bfloat16)
        xn16[0] = xn0
        u_ref[...] = (jnp.dot(xn0, win_ref[:, _O_U:_O_CB], preferred_element_type=jnp.float32)
                      + bin_ref[:, _O_U:_O_CB])

    @pl.when(step + 2 < n_steps)
    def _():
        for c in x_copies(step + 2):
            c.start()

    @pl.when(step + 1 < n_steps)
    def _():
        for c in x_copies(step + 1):
            c.wait()

    @pl.when(step >= 2)
    def _():
        for c in out_copies(oslot):
            c.wait()

    @pl.when(j == 0)
    def _():
        csr_ref[...] = jnp.zeros_like(csr_ref)
        csi_ref[...] = jnp.zeros_like(csi_ref)
        p_ref[0:halo, :] = jnp.zeros((halo, D_MODEL), jnp.float32)

    f32 = jnp.float32
    xn = xn16[oslot]

    u = u_ref[...]
    c_c = jnp.dot(xn, win_ref[:, _O_CC:_O_CV], preferred_element_type=f32) + bin_ref[:, _O_CC:_O_CV]
    u4 = u.reshape(N_CHUNKS, SSM_CHUNK, SUBLANES, SSM_WIDTH)
    piece = D_MODEL // N_BLOCKS
    ub = []
    for q in range(N_BLOCKS):
        ub.append(jnp.concatenate(
            [u4[:, jj, :, q * LANES:(q + 1) * LANES].reshape(CHUNK_ROWS, LANES) for jj in range(SSM_CHUNK)],
            axis=1).astype(jnp.bfloat16))
        v = jnp.dot(ub[q], bw_ref[q], preferred_element_type=f32)
        sr_ref[:, q * BLOCK_STATE:(q + 1) * BLOCK_STATE] = v[:, :BLOCK_STATE]
        si_ref[:, q * BLOCK_STATE:(q + 1) * BLOCK_STATE] = v[:, BLOCK_STATE:]
        cols = slice(_O_CV + q * piece, _O_CV + (q + 1) * piece)
        c_v = jnp.dot(xn, win_ref[:, cols], preferred_element_type=f32) + bin_ref[:, cols]
        p_ref[halo:halo + MIX_ROWS, q * piece:(q + 1) * piece] = c_c[:, q * piece:(q + 1) * piece] * c_v

    xn_next = _rmsnorm(xbuf[xnext].reshape(MIX_ROWS, D_MODEL), gmix_ref[...]).astype(jnp.bfloat16)
    xn16[1 - oslot] = xn_next

    for c in range(STATE_LANES // SCAN_LANES):
        sl = slice(c * SCAN_LANES, (c + 1) * SCAN_LANES)
        ar = jnp.broadcast_to(are_ref[:, sl], (SUBLANES, SCAN_LANES))
        ai = jnp.broadcast_to(aim_ref[:, sl], (SUBLANES, SCAN_LANES))
        cr, ci = csr_ref[:, sl], csi_ref[:, sl]
        for k in range(N_CHUNKS):
            rows = slice(k * SUBLANES, (k + 1) * SUBLANES)
            vr, vi = sr_ref[rows, sl], si_ref[rows, sl]
            sr_ref[rows, sl] = cr
            si_ref[rows, sl] = ci
            cr, ci = ar * cr - ai * ci + vr, ar * ci + ai * cr + vi
        csr_ref[:, sl] = cr
        csi_ref[:, sl] = ci

    conv = convb_ref[...]
    for k in range(CONV_K):
        conv = conv + convw_ref[k:k + 1, :] * p_ref[k * SUBLANES:k * SUBLANES + MIX_ROWS, :]
    p_ref[0:halo, :] = p_ref[MIX_ROWS:MIX_ROWS + halo, :]

    ys, cbc = [], []
    for q in range(N_BLOCKS):
        sq = jnp.concatenate([sr_ref[:, q * BLOCK_STATE:(q + 1) * BLOCK_STATE],
                              si_ref[:, q * BLOCK_STATE:(q + 1) * BLOCK_STATE]], axis=1)
        yb = jnp.dot(sq.astype(jnp.bfloat16), cw_ref[q], preferred_element_type=f32)
        yb = yb + jnp.dot(ub[q], tw_ref[q], preferred_element_type=f32)
        ys.append(jnp.concatenate(
            [yb[:, jj * LANES:(jj + 1) * LANES].reshape(N_CHUNKS, 1, SUBLANES, LANES)
             for jj in range(SSM_CHUNK)], axis=1).reshape(MIX_ROWS, LANES))
        cols = slice(_O_CB + q * piece, _O_CB + (q + 1) * piece)
        c_b = jnp.dot(xn, win_ref[:, cols], preferred_element_type=f32) + bin_ref[:, cols]
        cbc.append((c_b * conv[:, q * piece:(q + 1) * piece]).astype(jnp.bfloat16))
    cbc = jnp.concatenate(cbc, axis=1)
    y_ssm = jnp.concatenate(ys, axis=1) + dskip_ref[...] * u
    z = jax.nn.gelu(y_ssm, approximate=True).astype(jnp.bfloat16)
    g_ssm = jnp.dot(xn, win_ref[:, _O_GS:_O_GC], preferred_element_type=f32) + bin_ref[:, _O_GS:_O_GC]
    ab = jnp.dot(z, wglu_ref[...], preferred_element_type=f32)
    y_a = ab[:, :D_MODEL]
    y_a = y_a + y_a * jnp.tanh(ab[:, D_MODEL:])
    merged = y_a + y_a * jnp.tanh(g_ssm)
    y_b = jnp.dot(cbc, wco_ref[...], preferred_element_type=f32)
    g_conv = jnp.dot(xn, win_ref[:, _O_GC:IN_COLS], preferred_element_type=f32) + bin_ref[:, _O_GC:IN_COLS]
    merged = merged + (y_b + y_b * jnp.tanh(g_conv))

    mix = jnp.dot(merged.astype(jnp.bfloat16), wout_ref[...], preferred_element_type=f32)
    u_ref[...] = jnp.dot(xn_next, win_ref[:, _O_U:_O_CB], preferred_element_type=f32) + bin_ref[:, _O_U:_O_CB]
    obuf[oslot] = (xbuf[xslot].reshape(MIX_ROWS, D_MODEL) + mix).reshape(SEQ_TILE, BATCH_TILE, D_MODEL)

    for c in out_copies(oslot):
        c.start()

    @pl.when(step == n_steps - 1)
    def _():
        for c in out_copies(oslot) + out_copies(1 - oslot):
            c.wait()


def _resident(shape):
    return pl.BlockSpec(shape, lambda *_: (0,) * len(shape), pipeline_mode=pl.Buffered(1))


def _mixer(x, gmix, win, b_in, a_re, a_im, bw, cw, tw, dskip, wglu, convw, convb, wco, wout):
    batch, seq, d = x.shape
    assert d == D_MODEL and batch % BATCH_TILE == 0 and seq % SEQ_TILE == 0
    weights = (gmix, win, b_in, a_re, a_im, bw, cw, tw, dskip, wglu, convw, convb, wco, wout)
    tile = (SEQ_TILE, BATCH_TILE, D_MODEL)
    assert (batch // BATCH_TILE) * (seq // SEQ_TILE) >= X_SLOTS, "the rings assume at least X_SLOTS grid steps"
    return pl.pallas_call(
        _mixer_kernel,
        out_shape=jax.ShapeDtypeStruct(x.shape, jnp.float32),
        grid=(batch // BATCH_TILE, seq // SEQ_TILE),
        in_specs=[pl.BlockSpec(memory_space=pl.ANY)] + [_resident(w.shape) for w in weights],
        out_specs=pl.BlockSpec(memory_space=pl.ANY),
        scratch_shapes=[
            pltpu.VMEM((X_SLOTS, *tile), jnp.float32),
            pltpu.VMEM((2, *tile), jnp.float32),
            pltpu.SemaphoreType.DMA((X_SLOTS,)),
            pltpu.SemaphoreType.DMA((2,)),
            pltpu.VMEM((2, MIX_ROWS, D_MODEL), jnp.bfloat16),
            pltpu.VMEM((MIX_ROWS, SSM_WIDTH), jnp.float32),
            pltpu.VMEM((CHUNK_ROWS, STATE_LANES), jnp.float32),
            pltpu.VMEM((CHUNK_ROWS, STATE_LANES), jnp.float32),
            pltpu.VMEM((SUBLANES, STATE_LANES), jnp.float32),
            pltpu.VMEM((SUBLANES, STATE_LANES), jnp.float32),
            pltpu.VMEM((MIX_ROWS + (CONV_K - 1) * SUBLANES, D_MODEL), jnp.float32),
        ],
        compiler_params=pltpu.CompilerParams(
            dimension_semantics=("arbitrary", "arbitrary"),
            vmem_limit_bytes=VMEM_LIMIT_BYTES),
        name="mixer",
    )(x, *weights)


def _mlp_kernel(h_ref, gmlp_ref, w1_ref, w2_ref, gfin_ref, o_ref):
    h = h_ref[...]
    hn = _rmsnorm(h, gmlp_ref[...]).astype(jnp.bfloat16)
    acc = h
    n_chunks = D_FF // FF_CHUNK
    for c in range(n_chunks - 1):
        a = jnp.maximum(_dot(hn, w1_ref[:, c * FF_CHUNK:(c + 1) * FF_CHUNK]), 0.0)
        acc = acc + _dot((a * a).astype(jnp.bfloat16), w2_ref[c * FF_CHUNK:(c + 1) * FF_CHUNK, :])
    c = n_chunks - 1
    a = jnp.maximum(_dot(hn, w1_ref[:, c * FF_CHUNK:(c + 1) * FF_CHUNK]), 0.0)
    a = (a * a).astype(jnp.bfloat16)
    block = MLP_ROWS // MLP_TAIL_BLOCKS
    for r in range(MLP_TAIL_BLOCKS):
        rows = slice(r * block, (r + 1) * block)
        o_ref[rows, :] = _rmsnorm(acc[rows] + _dot(a[rows], w2_ref[c * FF_CHUNK:(c + 1) * FF_CHUNK, :]),
                                  gfin_ref[...])


def _mlp(h, gmlp, w1, w2, gfin):
    rows, d = h.shape
    assert d == D_MODEL and rows % MLP_ROWS == 0
    tile = pl.BlockSpec((MLP_ROWS, D_MODEL), lambda i: (i, 0))
    weights = (gmlp, w1, w2, gfin)
    in_specs = [tile, _resident(gmlp.shape), _resident(w1.shape), _resident(w2.shape), _resident(gfin.shape)]
    return pl.pallas_call(
        _mlp_kernel,
        out_shape=jax.ShapeDtypeStruct(h.shape, jnp.float32),
        grid=(rows // MLP_ROWS,),
        in_specs=in_specs,
        out_specs=tile,
        compiler_params=pltpu.CompilerParams(
            dimension_semantics=("arbitrary",),
            vmem_limit_bytes=VMEM_LIMIT_BYTES),
        name="mlp",
    )(h, *weights)


def _ssm_weights(lam_re, lam_im, log_dt, b_re, b_im, c_re, c_im):
    n = SSM_CHUNK
    dt = jnp.exp(log_dt)[:, None]

    def a_pow(m):
        m = jnp.asarray(m, jnp.float32)[..., None, None]
        mag = jnp.exp(m * (lam_re * dt))
        return mag * jnp.cos(m * (lam_im * dt)), mag * jnp.sin(m * (lam_im * dt))

    ab_re, ab_im = a_pow(1)
    er, ei = ab_re - 1.0, ab_im
    den = lam_re * lam_re + lam_im * lam_im
    q_re = (er * lam_re + ei * lam_im) / den
    q_im = (ei * lam_re - er * lam_im) / den
    bb_re = q_re[..., None] * b_re - q_im[..., None] * b_im
    bb_im = q_re[..., None] * b_im + q_im[..., None] * b_re

    pr, pi = a_pow(n - 1 - np.arange(n))
    pr, pi = pr[..., None], pi[..., None]
    bw = _group_block_diag(jnp.stack([pr * bb_re - pi * bb_im, pr * bb_im + pi * bb_re]))

    pr, pi = a_pow(np.arange(n + 1))
    pr, pi = pr[..., None], pi[..., None]
    c_re, c_im = c_re.transpose(0, 2, 1), c_im.transpose(0, 2, 1)
    cw_all = _group_block_diag(jnp.stack([c_re * pr - c_im * pi, -(c_re * pi + c_im * pr)]))
    cw_all = cw_all.transpose(0, 2, 1)

    k = jnp.einsum('qrs,qsc->qrc', bw[:, (n - 1) * LANES:, :], cw_all[:, :, :n * LANES],
                   precision=lax.Precision.HIGHEST)
    tw = jnp.concatenate(
        [jnp.concatenate([jnp.zeros((N_BLOCKS, LANES, i * LANES), k.dtype), k[:, :, :(n - i) * LANES]], axis=2)
         for i in range(n)], axis=1)
    an_re, an_im = a_pow(n)
    return an_re, an_im, bw, cw_all[:, :, LANES:], tw


def _group_block_diag(vals):
    parts, steps = vals.shape[:2]
    pair = LANES // SSM_STATE
    x = vals.reshape(parts, steps, N_BLOCKS, GROUPS_PER_BLOCK, SSM_STATE, SSM_GROUP)
    x = x.transpose(2, 1, 3, 5, 0, 4)
    half = np.arange(GROUPS_PER_BLOCK)[:, None] % pair == np.arange(pair)[None, :]
    x = jnp.where(half[None, None, :, None, None, :, None], x[:, :, :, :, :, None, :], 0.0)
    x = x.reshape(N_BLOCKS, steps * LANES, parts, LANES)
    tiled = jnp.concatenate(
        [x[:, :, p, :] for p in range(parts) for _ in range(BLOCK_STATE // LANES)], axis=-1)
    row_pair = (np.arange(steps * LANES) % LANES) // (pair * SSM_GROUP)
    col_tile = (np.arange(parts * BLOCK_STATE) % BLOCK_STATE) // LANES
    return jnp.where(row_pair[:, None] == col_tile[None, :], tiled, 0.0)


def _layer(h, norm_mix_g, w_in, b_in, lam_re, lam_im, log_dt, ssm_b_re, ssm_b_im, ssm_c_re, ssm_c_im,
           ssm_d, w_glu_a, w_glu_b, conv_w, conv_b, w_conv_out, w_out):
    bf16 = jnp.bfloat16
    an_re, an_im, bw, cw, tw = _ssm_weights(lam_re, lam_im, log_dt, ssm_b_re, ssm_b_im, ssm_c_re, ssm_c_im)
    gate_half = np.where(np.arange(IN_COLS) >= _O_GS, 0.5, 1.0).astype(np.float32)[None, :]
    return _mixer(
        h, norm_mix_g.reshape(1, D_MODEL), (w_in * gate_half).astype(bf16), b_in.reshape(1, IN_COLS) * gate_half,
        an_re.reshape(1, STATE_LANES), an_im.reshape(1, STATE_LANES),
        bw.astype(bf16), cw.astype(bf16), tw.astype(bf16),
        ssm_d.reshape(1, SSM_WIDTH), jnp.concatenate([0.25 * w_glu_a, 0.5 * w_glu_b], axis=1).astype(bf16),
        conv_w, conv_b.reshape(1, D_MODEL), (0.5 * w_conv_out).astype(bf16), w_out.astype(bf16))


def kernel(x, norm_mix_g, w_in, b_in, lam_re, lam_im, log_dt, ssm_b_re, ssm_b_im, ssm_c_re, ssm_c_im,
           ssm_d, w_glu_a, w_glu_b, conv_w, conv_b, w_conv_out, w_out, norm_mlp_g, w_ff1, w_ff2,
           norm_final_g):
    depth = norm_mix_g.shape[0]
    assert depth == 1, "the MLP call applies the final norm, so it must be the last layer"
    batch, seq, d = x.shape
    l = 0
    h = _layer(x, norm_mix_g[l], w_in[l], b_in[l], lam_re[l], lam_im[l], log_dt[l],
               ssm_b_re[l], ssm_b_im[l], ssm_c_re[l], ssm_c_im[l], ssm_d[l],
               w_glu_a[l], w_glu_b[l], conv_w[l], conv_b[l], w_conv_out[l], w_out[l])
    out = _mlp(h.reshape(batch * seq, d), norm_mlp_g[l].reshape(1, d), w_ff1[l].astype(jnp.bfloat16),
               w_ff2[l].astype(jnp.bfloat16), norm_final_g.reshape(1, d))
    return out.reshape(batch, seq, d)
```
